```python
import jax, jax.numpy as jnp
from jax import lax
import numpy as np

D_MODEL = 1024
BATCH = 8
SEQ = 4096
DEPTH = 4

N_META = 16
N_A_LAYERS = DEPTH // 2
N_B_LAYERS = DEPTH - N_A_LAYERS
RET_HEADS = 4
RET_QK_DIM = D_MODEL // RET_HEADS
RET_V_DIM = 2 * D_MODEL // RET_HEADS
RET_CHUNK = 128
RET_PROJ = 2 * RET_HEADS * RET_QK_DIM + 2 * RET_HEADS * RET_V_DIM
SWA_HEADS = 16
SWA_KV_HEADS = 4
SWA_GROUP = SWA_HEADS // SWA_KV_HEADS
SWA_HEAD_DIM = 64
SWA_WINDOW = 128
FFN_HIDDEN = -(-8 * D_MODEL // (3 * 256)) * 256
RMS_EPS = 1e-6
GN_EPS = 1e-6

kernel_name = "retnet_yoco_swa_sink_alibi_meta_trunk"


def rms_norm(x, g):
    xf = x.astype(jnp.float32)
    y = xf * lax.rsqrt(jnp.mean(xf * xf, axis=-1, keepdims=True) + RMS_EPS)
    return (y * g.astype(jnp.float32)).astype(x.dtype)


def swiglu(x, w_in, w_out):
    gate, up = jnp.split(x @ w_in, 2, axis=-1)
    return (jax.nn.silu(gate) * up) @ w_out


def retention_log_decay():
    return jnp.log1p(-jnp.exp2(-5.0 - jnp.arange(RET_HEADS, dtype=jnp.float32)))


def alibi_slopes():
    return jnp.exp2(-8.0 * (jnp.arange(SWA_HEADS, dtype=jnp.float32) + 1.0) / SWA_HEADS)


def retention(x, w_in, w_out):
    B, L, _ = x.shape
    H, dk, dv, C = RET_HEADS, RET_QK_DIM, RET_V_DIM, RET_CHUNK
    pad = C - N_META
    nc = (L + pad) // C
    q, k, v, g = jnp.split(x @ w_in, [H * dk, 2 * H * dk, 2 * H * dk + H * dv], axis=-1)

    def chunks(t, d):
        return jnp.pad(t, ((0, 0), (pad, 0), (0, 0))).reshape(B, nc, C, H, d)

    q = chunks(q, dk) * (dk ** -0.5)
    k = chunks(k, dk)
    v = chunks(v, dv)
    log_gamma = retention_log_decay()
    i = jnp.arange(C, dtype=jnp.float32)
    diff = i[:, None] - i[None, :]
    decay_intra = jnp.where(diff >= 0, jnp.exp(log_gamma[:, None, None] * jnp.maximum(diff, 0.0)), 0.0)
    scores = jnp.einsum('bnchd,bnshd->bnhcs', q, k) * decay_intra
    intra = jnp.einsum('bnhcs,bnshe->bnche', scores, v)
    zeta = jnp.exp(log_gamma[:, None] * (C - 1.0 - i)[None, :])
    xi = jnp.exp(log_gamma[:, None] * (i + 1.0)[None, :])
    chunk_decay = jnp.exp(log_gamma * C)[None, :, None, None]

    def step(state, xs):
        qc, kc, vc = xs
        inter = jnp.einsum('bchd,hc,bhde->bche', qc, xi, state)
        state = state * chunk_decay + jnp.einsum('bshd,hs,bshe->bhde', kc, zeta, vc)
        return state, inter

    state0 = jnp.zeros((B, H, dk, dv), jnp.float32)
    _, inter = lax.scan(step, state0, (jnp.moveaxis(q, 1, 0), jnp.moveaxis(k, 1, 0), jnp.moveaxis(v, 1, 0)))
    o = intra + jnp.moveaxis(inter, 0, 1)
    o = o.reshape(B, nc * C, H, dv)[:, pad:].astype(jnp.float32)
    mu = jnp.mean(o, axis=-1, keepdims=True)
    var = jnp.mean(jnp.square(o - mu), axis=-1, keepdims=True)
    o = ((o - mu) * lax.rsqrt(var + GN_EPS)).astype(x.dtype).reshape(B, L, H * dv)
    return (jax.nn.silu(g) * o) @ w_out


def shared_kv(h, g, w_kv):
    B, L, _ = h.shape
    k, v = jnp.split(rms_norm(h, g) @ w_kv, 2, axis=-1)
    k = k.reshape(B, L, SWA_KV_HEADS, SWA_HEAD_DIM)
    v = v.reshape(B, L, SWA_KV_HEADS, SWA_HEAD_DIM)
    return k[:, :N_META], v[:, :N_META], k[:, N_META:], v[:, N_META:]


def sliding_window_attention(x, w_q, w_o, sinks, k_meta, v_meta, k_real, v_real):
    B, S, _ = x.shape
    W = SWA_WINDOW
    nb = S // W
    q = (x @ w_q).reshape(B, nb, W, SWA_KV_HEADS, SWA_GROUP, SWA_HEAD_DIM) * (SWA_HEAD_DIM ** -0.5)

    def band(t):
        tp = jnp.pad(t, ((0, 0), (W, 0), (0, 0), (0, 0))).reshape(B, nb + 1, W, SWA_KV_HEADS, SWA_HEAD_DIM)
        return jnp.concatenate([tp[:, :-1], tp[:, 1:]], axis=2)

    k_band, v_band = band(k_real), band(v_real)
    slopes = alibi_slopes().reshape(SWA_KV_HEADS, SWA_GROUP)[:, :, None, None, None]
    r = jnp.arange(W)
    c = jnp.arange(2 * W)
    blk = jnp.arange(nb)
    dist = W + r[:, None] - c[None, :]
    mask = (dist >= 0) & (dist < W) & ((blk[:, None, None] * W + c[None, None, :]) >= W)
    s_band = jnp.einsum('bnrkgd,bnckd->bkgnrc', q, k_band).astype(jnp.float32) - slopes * dist.astype(jnp.float32)
    s_band = jnp.where(mask, s_band, -jnp.inf)
    q_pos = N_META + blk[:, None] * W + r[None, :]
    meta_dist = (q_pos[:, :, None] - jnp.arange(N_META)[None, None, :]).astype(jnp.float32)
    s_meta = jnp.einsum('bnrkgd,bmkd->bkgnrm', q, k_meta).astype(jnp.float32) - slopes * meta_dist
    s_sink = sinks.astype(jnp.float32).reshape(SWA_KV_HEADS, SWA_GROUP)[None, :, :, None, None]
    m = jnp.maximum(jnp.maximum(s_band.max(-1), s_meta.max(-1)), s_sink)
    e_band = jnp.exp(s_band - m[..., None])
    e_meta = jnp.exp(s_meta - m[..., None])
    inv = 1.0 / (e_band.sum(-1) + e_meta.sum(-1) + jnp.exp(s_sink - m))
    p_band = (e_band * inv[..., None]).astype(v_band.dtype)
    p_meta = (e_meta * inv[..., None]).astype(v_meta.dtype)
    o = (jnp.einsum('bkgnrc,bnckd->bnrkgd', p_band, v_band)
         + jnp.einsum('bkgnrm,bmkd->bnrkgd', p_meta, v_meta))
    return o.reshape(B, S, SWA_HEADS * SWA_HEAD_DIM).astype(x.dtype) @ w_o


def setup_inputs(seed: int = 0) -> dict:
    key = jax.random.key(seed)
    ks = jax.random.split(key, 14)
    resid_scale = (2.0 * DEPTH) ** -0.5

    def nrm(k, shape, fan_in, scale=1.0):
        return jax.random.normal(k, shape, jnp.float32) * (scale * fan_in ** -0.5)

    def gain(k, shape):
        return 1.0 + 0.02 * jax.random.normal(k, shape, jnp.float32)

    return {
        "x": jax.random.normal(ks[0], (BATCH, SEQ, D_MODEL), jnp.float32),
        "meta_tokens": jax.random.normal(ks[1], (N_META, D_MODEL), jnp.float32),
        "mix_norm": gain(ks[2], (DEPTH, D_MODEL)),
        "ffn_norm": gain(ks[3], (DEPTH, D_MODEL)),
        "ret_w_in": nrm(ks[4], (N_A_LAYERS, D_MODEL, RET_PROJ), D_MODEL),
        "ret_w_out": nrm(ks[5], (N_A_LAYERS, RET_HEADS * RET_V_DIM, D_MODEL), RET_HEADS * RET_V_DIM, resid_scale),
        "kv_norm": gain(ks[6], (D_MODEL,)),
        "kv_w": nrm(ks[7], (D_MODEL, 2 * SWA_KV_HEADS * SWA_HEAD_DIM), D_MODEL),
        "swa_w_q": nrm(ks[8], (N_B_LAYERS, D_MODEL, SWA_HEADS * SWA_HEAD_DIM), D_MODEL),
        "swa_w_o": nrm(ks[9], (N_B_LAYERS, SWA_HEADS * SWA_HEAD_DIM, D_MODEL), SWA_HEADS * SWA_HEAD_DIM, resid_scale),
        "swa_sinks": 0.5 * jax.random.normal(ks[10], (N_B_LAYERS, SWA_HEADS), jnp.float32),
        "ffn_w_in": nrm(ks[11], (DEPTH, D_MODEL, 2 * FFN_HIDDEN), D_MODEL),
        "ffn_w_out": nrm(ks[12], (DEPTH, FFN_HIDDEN, D_MODEL), FFN_HIDDEN, resid_scale),
        "final_norm": gain(ks[13], (D_MODEL,)),
    }


def reference(x, meta_tokens, mix_norm, ffn_norm, ret_w_in, ret_w_out, kv_norm, kv_w,
              swa_w_q, swa_w_o, swa_sinks, ffn_w_in, ffn_w_out, final_norm):
    B = x.shape[0]
    meta = jnp.broadcast_to(meta_tokens.astype(x.dtype)[None], (B, N_META, D_MODEL))
    h = jnp.concatenate([meta, x], axis=1)
    k_meta = v_meta = k_real = v_real = None
    for layer in range(DEPTH):
        if layer < N_A_LAYERS:
            h = h + retention(rms_norm(h, mix_norm[layer]), ret_w_in[layer], ret_w_out[layer])
        else:
            if layer == N_A_LAYERS:
                k_meta, v_meta, k_real, v_real = shared_kv(h, kv_norm, kv_w)
                h = h[:, N_META:]
            b = layer - N_A_LAYERS
            h = h + sliding_window_attention(rms_norm(h, mix_norm[layer]), swa_w_q[b], swa_w_o[b], swa_sinks[b],
                                             k_meta, v_meta, k_real, v_real)
        h = h + swiglu(rms_norm(h, ffn_norm[layer]), ffn_w_in[layer], ffn_w_out[layer])
    return rms_norm(h, final_norm)
```

```python
import functools

import numpy as np
import jax
import jax.numpy as jnp
from jax import lax
from jax.experimental import pallas as pl
from jax.experimental.pallas import tpu as pltpu

D_MODEL = 1024
N_META = 16
RET_HEADS = 4
RET_QK_DIM = 256
RET_V_DIM = 512
RET_CHUNK = 128
RET_PROJ = 2 * RET_HEADS * RET_QK_DIM + 2 * RET_HEADS * RET_V_DIM
SWA_HEADS = 16
SWA_KV_HEADS = 4
SWA_GROUP = 4
SWA_HEAD_DIM = 64
SWA_WINDOW = 128
SWA_KV_DIM = SWA_KV_HEADS * SWA_HEAD_DIM
FFN_HIDDEN = 2816
FFN_COL_CHUNK = 256
RMS_EPS = 1e-6
GN_EPS = 1e-6

TOKEN_TILE = 512
META_TILE = RET_CHUNK
VMEM_LIMIT_BYTES = 56 * 1024 * 1024

BF16 = jnp.bfloat16
F32 = jnp.float32
NEG_INF = float("-inf")


def _resident(block_shape):
    zeros = (0,) * len(block_shape)
    return pl.BlockSpec(block_shape, lambda *_: zeros, pipeline_mode=pl.Buffered(1))


def _rms_norm_bf16(x, gain):
    ms = jnp.mean(x * x, axis=-1, keepdims=True)
    return (x * lax.rsqrt(ms + RMS_EPS) * gain).astype(BF16)


def _dot(a, b):
    return jnp.dot(a, b, preferred_element_type=F32)


def _dot_nt(a, b):
    return lax.dot_general(a, b, (((1,), (1,)), ((), ())), preferred_element_type=F32)


def _dot_tn(a, b):
    return lax.dot_general(a, b, (((0,), (0,)), ((), ())), preferred_element_type=F32)


def _ret_tables():
    C = RET_CHUNK
    log_gamma = jnp.log1p(-jnp.exp2(-5.0 - jnp.arange(RET_HEADS, dtype=F32)))
    i = jnp.arange(C, dtype=F32)
    diff = i[:, None] - i[None, :]
    dmask = jnp.where(diff >= 0, jnp.exp(log_gamma[:, None, None] * jnp.maximum(diff, 0.0)), 0.0)
    zeta = jnp.exp(log_gamma[:, None] * (C - 1.0 - i)[None, :])
    xi = jnp.exp(log_gamma[:, None] * (i + 1.0)[None, :])
    chunk_decay = jnp.exp(log_gamma * C)
    xi = jnp.broadcast_to(xi[:, :, None], (RET_HEADS, C, RET_QK_DIM))
    zeta = jnp.broadcast_to(zeta[:, :, None], (RET_HEADS, C, RET_QK_DIM))
    cd = jnp.broadcast_to(chunk_decay[:, None, None], (RET_HEADS, 8, RET_V_DIM))
    return dmask, xi, zeta, cd


def _ret_kernel(h_ref, gain_ref, win_ref, wout_ref, dmask_ref, xi_ref, zeta_ref, cd_ref, state0_ref,
                *rest, n_chunks, emit_state):
    if emit_state:
        o_ref, state_out_ref, state_ref, proj_ref, og_ref = rest
    else:
        o_ref, state_ref, proj_ref, og_ref = rest
    H, dk, dv, C = RET_HEADS, RET_QK_DIM, RET_V_DIM, RET_CHUNK
    t = pl.program_id(1)

    @pl.when(t == 0)
    def _():
        state_ref[...] = state0_ref[...]

    x = h_ref[0]
    proj_ref[...] = _dot(_rms_norm_bf16(x, gain_ref[...]), win_ref[...])
    scale = dk ** -0.5
    for c in range(n_chunks):
        rows = slice(c * C, (c + 1) * C)
        for hh in range(H):
            q = proj_ref[rows, hh * dk:(hh + 1) * dk] * scale
            k = proj_ref[rows, H * dk + hh * dk:H * dk + (hh + 1) * dk]
            v = proj_ref[rows, 2 * H * dk + hh * dv:2 * H * dk + (hh + 1) * dv].astype(BF16)
            gate = proj_ref[rows, 2 * H * dk + H * dv + hh * dv:2 * H * dk + H * dv + (hh + 1) * dv]
            scores = _dot_nt(q.astype(BF16), k.astype(BF16)) * dmask_ref[hh]
            intra = _dot(scores.astype(BF16), v)
            state = state_ref[hh]
            inter = _dot((q * xi_ref[hh]).astype(BF16), state.astype(BF16))
            kz = (k * zeta_ref[hh]).astype(BF16)
            state_ref[hh] = state * cd_ref[hh][0:1, :] + _dot_tn(kz, v)
            o = intra + inter
            mu = jnp.mean(o, axis=-1, keepdims=True)
            oc = o - mu
            var = jnp.mean(oc * oc, axis=-1, keepdims=True)
            on = oc * lax.rsqrt(var + GN_EPS)
            og_ref[rows, hh * dv:(hh + 1) * dv] = (gate * jax.nn.sigmoid(gate) * on).astype(BF16)
    o_ref[0] = x + _dot(og_ref[...], wout_ref[...])

    if emit_state:
        @pl.when(t == pl.num_programs(1) - 1)
        def _():
            state_out_ref[0] = state_ref[...]


def _ret_mixer(h, gain, w_in, w_out, tables, state0, *, tile, emit_state):
    B, L, D = h.shape
    H, dk, dv, C = RET_HEADS, RET_QK_DIM, RET_V_DIM, RET_CHUNK
    dmask, xi, zeta, cd = tables
    out_shape = [jax.ShapeDtypeStruct((B, L, D), F32)]
    out_specs = [pl.BlockSpec((1, tile, D), lambda b, t: (b, t, 0))]
    if emit_state:
        out_shape.append(jax.ShapeDtypeStruct((B, H, dk, dv), F32))
        out_specs.append(pl.BlockSpec((1, H, dk, dv), lambda b, t: (b, 0, 0, 0)))
    res = pl.pallas_call(
        functools.partial(_ret_kernel, n_chunks=tile // C, emit_state=emit_state),
        grid=(B, L // tile),
        in_specs=[
            pl.BlockSpec((1, tile, D), lambda b, t: (b, t, 0)),
            _resident((1, D)),
            _resident((D, RET_PROJ)),
            _resident((H * dv, D)),
            _resident((H, C, C)),
            _resident((H, C, dk)),
            _resident((H, C, dk)),
            _resident((H, 8, dv)),
            _resident((H, dk, dv)),
        ],
        out_specs=out_specs,
        out_shape=out_shape,
        scratch_shapes=[
            pltpu.VMEM((H, dk, dv), F32),
            pltpu.VMEM((tile, RET_PROJ), F32),
            pltpu.VMEM((tile, H * dv), BF16),
        ],
        compiler_params=pltpu.CompilerParams(
            dimension_semantics=("arbitrary", "arbitrary"), vmem_limit_bytes=VMEM_LIMIT_BYTES),
        name="ret_mixer_meta" if emit_state else "ret_mixer",
    )(h, gain, w_in, w_out, dmask, xi, zeta, cd, state0)
    return res if emit_state else res[0]


def _ffn_kernel(h_ref, gain_ref, win_ref, wout_ref, *rest, final_norm):
    if final_norm:
        fgain_ref, o_ref, act_ref = rest
    else:
        o_ref, act_ref = rest
    F, CH = FFN_HIDDEN, FFN_COL_CHUNK
    x = h_ref[...]
    xn = _rms_norm_bf16(x, gain_ref[...])
    for j in range(F // CH):
        gate = _dot(xn, win_ref[:, j * CH:(j + 1) * CH])
        up = _dot(xn, win_ref[:, F + j * CH:F + (j + 1) * CH])
        act_ref[:, j * CH:(j + 1) * CH] = (gate * jax.nn.sigmoid(gate) * up).astype(BF16)
    y = x + _dot(act_ref[...], wout_ref[...])
    if final_norm:
        ms = jnp.mean(y * y, axis=-1, keepdims=True)
        y = y * lax.rsqrt(ms + RMS_EPS) * fgain_ref[...]
    o_ref[...] = y


def _ffn(h, gain, w_in, w_out, *, tile, final_gain=None):
    T, D = h.shape
    F = FFN_HIDDEN
    final_norm = final_gain is not None
    in_specs = [
        pl.BlockSpec((tile, D), lambda t: (t, 0)),
        _resident((1, D)),
        _resident((D, 2 * F)),
        _resident((F, D)),
    ]
    args = [h, gain, w_in, w_out]
    if final_norm:
        in_specs.append(_resident((1, D)))
        args.append(final_gain)
    return pl.pallas_call(
        functools.partial(_ffn_kernel, final_norm=final_norm),
        grid=(T // tile,),
        in_specs=in_specs,
        out_specs=pl.BlockSpec((tile, D), lambda t: (t, 0)),
        out_shape=jax.ShapeDtypeStruct((T, D), F32),
        scratch_shapes=[pltpu.VMEM((tile, F), BF16)],
        compiler_params=pltpu.CompilerParams(
            dimension_semantics=("arbitrary",), vmem_limit_bytes=VMEM_LIMIT_BYTES),
        name="ffn_final" if final_norm else "ffn",
    )(*args)


def _kv_kernel(h_ref, gain_ref, w_ref, o_ref):
    o_ref[...] = _dot(_rms_norm_bf16(h_ref[...], gain_ref[...]), w_ref[...]).astype(BF16)


def _kv_proj(h, gain, w, *, tile):
    T, D = h.shape
    N = w.shape[1]
    return pl.pallas_call(
        _kv_kernel,
        grid=(T // tile,),
        in_specs=[pl.BlockSpec((tile, D), lambda t: (t, 0)), _resident((1, D)), _resident((D, N))],
        out_specs=pl.BlockSpec((tile, N), lambda t: (t, 0)),
        out_shape=jax.ShapeDtypeStruct((T, N), BF16),
        compiler_params=pltpu.CompilerParams(
            dimension_semantics=("arbitrary",), vmem_limit_bytes=VMEM_LIMIT_BYTES),
        name="kv_proj",
    )(h, gain, w)


def _swa_bias_tables():
    W = SWA_WINDOW
    slopes = jnp.exp2(-8.0 * (jnp.arange(SWA_HEADS, dtype=F32) + 1.0) / SWA_HEADS)
    r = jnp.arange(W)[:, None]
    c = jnp.arange(W)[None, :]
    d_prev = (W + r - c).astype(F32)
    d_cur = (r - c).astype(F32)
    s = slopes[:, None, None]
    b_prev = jnp.where((c > r)[None], -s * d_prev[None], NEG_INF)
    b_cur = jnp.where((c <= r)[None], -s * d_cur[None], NEG_INF)
    lane = jnp.arange(W)[None, :]
    lane_kv = lane // N_META
    lane_m = lane % N_META
    d_meta = (N_META + r - lane_m).astype(F32)
    head_kv = (jnp.arange(SWA_HEADS) // SWA_GROUP)[:, None, None]
    b_meta = jnp.where(lane_kv[None] == head_kv, -s * d_meta[None], NEG_INF)
    return jnp.stack([b_prev, b_cur, b_meta], axis=1), slopes


def _swa_kernel(sinks_ref, slopes_ref, h_ref, gain_ref, wq_ref, wo_ref, kvp_ref, kvc_ref, kvm_ref,
                bias_ref, o_ref, q_ref, km_ref, vm_ref, kmeta_ref, vmeta_ref, sb_ref, sm_ref, pb_ref,
                pm_ref, att_ref, *, n_blocks):
    W, G, KVH, KD = SWA_WINDOW, SWA_GROUP, SWA_KV_HEADS, SWA_KV_DIM
    t = pl.program_id(1)
    x = h_ref[0]
    q_ref[...] = (_dot(_rms_norm_bf16(x, gain_ref[...]), wq_ref[...]) * (SWA_HEAD_DIM ** -0.5)).astype(BF16)

    col_kv = lax.broadcasted_iota(jnp.int32, (W, KD), 1) // SWA_HEAD_DIM
    col_kv_meta = lax.broadcasted_iota(jnp.int32, (N_META, KD), 1) // SWA_HEAD_DIM
    zero = jnp.zeros((), BF16)
    kmeta_ref[...] = jnp.zeros_like(kmeta_ref)
    vmeta_ref[...] = jnp.zeros_like(vmeta_ref)
    for kv in range(KVH):
        rows = slice(kv * N_META, (kv + 1) * N_META)
        kmeta_ref[rows, :] = jnp.where(col_kv_meta == kv, kvm_ref[:, :KD], zero)
        vmeta_ref[rows, :] = jnp.where(col_kv_meta == kv, kvm_ref[:, KD:], zero)
    for j in range(n_blocks + 1):
        blk = kvp_ref[0] if j == 0 else kvc_ref[0, (j - 1) * W:j * W, :]
        for kv in range(KVH):
            rows = slice(j * KVH * W + kv * W, j * KVH * W + (kv + 1) * W)
            km_ref[rows, :] = jnp.where(col_kv == kv, blk[:, :KD], zero)
            vm_ref[rows, :] = jnp.where(col_kv == kv, blk[:, KD:], zero)

    first_pen = jnp.where(t == 0, NEG_INF, 0.0).astype(F32)
    for b in range(n_blocks):
        qs = jnp.concatenate([q_ref[b * W:(b + 1) * W, g * KD:(g + 1) * KD] for g in range(G)], axis=0)
        key_rows = slice(b * KVH * W, (b + 2) * KVH * W)
        sb_ref[...] = _dot_nt(qs, km_ref[key_rows, :])
        sm_ref[...] = _dot_nt(qs, kmeta_ref[...])
        block_index = (t * n_blocks + b).astype(F32)
        for kv in range(KVH):
            for g in range(G):
                head = kv * G + g
                rows = slice(g * W, (g + 1) * W)
                s_prev = sb_ref[rows, kv * W:(kv + 1) * W] + bias_ref[head, 0]
                if b == 0:
                    s_prev = s_prev + first_pen
                s_cur = sb_ref[rows, KVH * W + kv * W:KVH * W + (kv + 1) * W] + bias_ref[head, 1]
                meta_shift = -(slopes_ref[head] * float(W)) * block_index
                s_meta = sm_ref[rows, :] + (bias_ref[head, 2] + meta_shift)
                sink = sinks_ref[head]
                m = jnp.max(jnp.maximum(jnp.maximum(s_prev, s_cur), s_meta), axis=-1, keepdims=True)
                m = jnp.maximum(m, sink)
                e_prev = jnp.exp(s_prev - m)
                e_cur = jnp.exp(s_cur - m)
                e_meta = jnp.exp(s_meta - m)
                denom = jnp.sum(e_prev + e_cur + e_meta, axis=-1, keepdims=True) + jnp.exp(sink - m)
                inv = 1.0 / denom
                pb_ref[rows, kv * W:(kv + 1) * W] = (e_prev * inv).astype(BF16)
                pb_ref[rows, KVH * W + kv * W:KVH * W + (kv + 1) * W] = (e_cur * inv).astype(BF16)
                pm_ref[rows, :] = (e_meta * inv).astype(BF16)
        out = _dot(pb_ref[...], vm_ref[key_rows, :]) + _dot(pm_ref[...], vmeta_ref[...])
        for g in range(G):
            att_ref[b * W:(b + 1) * W, g * KD:(g + 1) * KD] = out[g * W:(g + 1) * W, :].astype(BF16)
    o_ref[0] = x + _dot(att_ref[...], wo_ref[...])


def _swa_mixer(h, gain, wq, wo, sinks, slopes, kv, kv_meta, bias, *, tile):
    B, S, D = h.shape
    W, G, KVH, KD = SWA_WINDOW, SWA_GROUP, SWA_KV_HEADS, SWA_KV_DIM
    n_blocks = tile // W
    smem = pl.BlockSpec(memory_space=pltpu.SMEM)
    return pl.pallas_call(
        functools.partial(_swa_kernel, n_blocks=n_blocks),
        grid=(B, S // tile),
        in_specs=[
            smem, smem,
            pl.BlockSpec((1, tile, D), lambda b, t: (b, t, 0)),
            _resident((1, D)),
            _resident((D, SWA_HEADS * SWA_HEAD_DIM)),
            _resident((SWA_HEADS * SWA_HEAD_DIM, D)),
            pl.BlockSpec((1, W, 2 * KD), lambda b, t: (b, jnp.maximum(t * n_blocks - 1, 0), 0)),
            pl.BlockSpec((1, tile, 2 * KD), lambda b, t: (b, t, 0)),
            _resident((N_META, 2 * KD)),
            _resident((SWA_HEADS, 3, W, W)),
        ],
        out_specs=pl.BlockSpec((1, tile, D), lambda b, t: (b, t, 0)),
        out_shape=jax.ShapeDtypeStruct((B, S, D), F32),
        scratch_shapes=[
            pltpu.VMEM((tile, SWA_HEADS * SWA_HEAD_DIM), BF16),
            pltpu.VMEM(((n_blocks + 1) * KVH * W, KD), BF16),
            pltpu.VMEM(((n_blocks + 1) * KVH * W, KD), BF16),
            pltpu.VMEM((W, KD), BF16),
            pltpu.VMEM((W, KD), BF16),
            pltpu.VMEM((G * W, 2 * KVH * W), F32),
            pltpu.VMEM((G * W, W), F32),
            pltpu.VMEM((G * W, 2 * KVH * W), BF16),
            pltpu.VMEM((G * W, W), BF16),
            pltpu.VMEM((tile, SWA_HEADS * SWA_HEAD_DIM), BF16),
        ],
        compiler_params=pltpu.CompilerParams(
            dimension_semantics=("arbitrary", "arbitrary"), vmem_limit_bytes=VMEM_LIMIT_BYTES),
        name="swa_mixer",
    )(sinks, slopes, h, gain, wq, wo, kv, kv, kv_meta, bias)


def _group_major(w, axis):
    shape = w.shape
    split = shape[:axis] + (SWA_KV_HEADS, SWA_GROUP, SWA_HEAD_DIM) + shape[axis + 1:]
    return jnp.swapaxes(w.reshape(split), axis, axis + 1).reshape(shape)


def kernel(x, meta_tokens, mix_norm, ffn_norm, ret_w_in, ret_w_out, kv_norm, kv_w, swa_w_q, swa_w_o,
           swa_sinks, ffn_w_in, ffn_w_out, final_norm):
    B, S, D = x.shape
    n_ret = ret_w_in.shape[0]
    n_swa = swa_w_q.shape[0]
    depth = n_ret + n_swa
    assert S % TOKEN_TILE == 0 and (B, S, D) == (x.shape[0], S, D_MODEL)

    tables = _ret_tables()
    bias, slopes = _swa_bias_tables()
    row = lambda g: g.reshape(1, D).astype(F32)

    hm = jnp.concatenate([jnp.zeros((META_TILE - N_META, D), F32), meta_tokens.astype(F32)], axis=0)
    h = x.astype(F32)
    state0 = jnp.zeros((RET_HEADS, RET_QK_DIM, RET_V_DIM), F32)

    for layer in range(n_ret):
        w_in = ret_w_in[layer].astype(BF16)
        w_out = ret_w_out[layer].astype(BF16)
        f_in = ffn_w_in[layer].astype(BF16)
        f_out = ffn_w_out[layer].astype(BF16)
        hm, state_meta = _ret_mixer(hm[None], row(mix_norm[layer]), w_in, w_out, tables, state0,
                                    tile=META_TILE, emit_state=True)
        h = _ret_mixer(h, row(mix_norm[layer]), w_in, w_out, tables, state_meta[0],
                       tile=TOKEN_TILE, emit_state=False)
        hm = _ffn(hm[0], row(ffn_norm[layer]), f_in, f_out, tile=META_TILE)
        h = _ffn(h.reshape(B * S, D), row(ffn_norm[layer]), f_in, f_out, tile=TOKEN_TILE).reshape(B, S, D)

    kv_w16 = kv_w.astype(BF16)
    kv_meta = _kv_proj(hm, row(kv_norm), kv_w16, tile=META_TILE)[META_TILE - N_META:]
    kv = _kv_proj(h.reshape(B * S, D), row(kv_norm), kv_w16, tile=TOKEN_TILE).reshape(B, S, 2 * SWA_KV_DIM)

    for b in range(n_swa):
        layer = n_ret + b
        wq = _group_major(swa_w_q[b], 1).astype(BF16)
        wo = _group_major(swa_w_o[b], 0).astype(BF16)
        h = _swa_mixer(h, row(mix_norm[layer]), wq, wo, swa_sinks[b].astype(F32), slopes, kv, kv_meta, bias,
                       tile=TOKEN_TILE)
        final_gain = row(final_norm) if layer == depth - 1 else None
        h = _ffn(h.reshape(B * S, D), row(ffn_norm[layer]), ffn_w_in[layer].astype(BF16),
                 ffn_w_out[layer].astype(BF16), tile=TOKEN_TILE, final_gain=final_gain).reshape(B, S, D)
    return h
```

```python
import functools

import jax
import jax.numpy as jnp
from jax import lax
from jax.experimental import pallas as pl
from jax.experimental.pallas import tpu as pltpu

D_MODEL = 1024
N_META = 16
RET_HEADS = 4
RET_QK_DIM = 256
RET_V_DIM = 512
RET_CHUNK = 128
RET_PROJ = 2 * RET_HEADS * RET_QK_DIM + 2 * RET_HEADS * RET_V_DIM
SWA_HEADS = 16
SWA_KV_HEADS = 4
SWA_GROUP = 4
SWA_HEAD_DIM = 64
SWA_WINDOW = 128
SWA_Q_DIM = SWA_HEADS * SWA_HEAD_DIM
SWA_KV_DIM = SWA_KV_HEADS * SWA_HEAD_DIM
FFN_HIDDEN = 2816
FFN_COL_CHUNK = 256
FFN_OUT_GROUP = 4
RMS_EPS = 1e-6
GN_EPS = 1e-6

TOKEN_TILE = 512
META_TILE = RET_CHUNK
VMEM_LIMIT_BYTES = 56 * 1024 * 1024

BF16 = jnp.bfloat16
F32 = jnp.float32
NEG_INF = float("-inf")


def _resident(block_shape):
    zeros = (0,) * len(block_shape)
    return pl.BlockSpec(block_shape, lambda *_: zeros, pipeline_mode=pl.Buffered(1))


def _resident_layer(layer, tail_shape):
    index = (layer,) + (0,) * len(tail_shape)
    return pl.BlockSpec((None,) + tuple(tail_shape), lambda *_: index, pipeline_mode=pl.Buffered(1))


def _rms_norm(x, gain):
    ms = jnp.mean(x * x, axis=-1, keepdims=True)
    return x * lax.rsqrt(ms + RMS_EPS) * gain


def _dot(a, b):
    return jnp.dot(a, b, preferred_element_type=F32)


def _dot_nt(a, b):
    return lax.dot_general(a, b, (((1,), (1,)), ((), ())), preferred_element_type=F32)


def _dot_tn(a, b):
    return lax.dot_general(a, b, (((0,), (0,)), ((), ())), preferred_element_type=F32)


def _params(n_grid_axes):
    return pltpu.CompilerParams(dimension_semantics=("arbitrary",) * n_grid_axes,
                                vmem_limit_bytes=VMEM_LIMIT_BYTES)


def _ret_tables():
    C = RET_CHUNK
    log_gamma = jnp.log1p(-jnp.exp2(-5.0 - jnp.arange(RET_HEADS, dtype=F32)))
    i = jnp.arange(C, dtype=F32)
    diff = i[:, None] - i[None, :]
    dmask = jnp.where(diff >= 0, jnp.exp(log_gamma[:, None, None] * jnp.maximum(diff, 0.0)), 0.0)
    zeta = jnp.exp(log_gamma[:, None] * (C - 1.0 - i)[None, :])
    xi = jnp.exp(log_gamma[:, None] * (i + 1.0)[None, :])
    chunk_decay = jnp.exp(log_gamma * C)
    xi = jnp.broadcast_to(xi[:, :, None], (RET_HEADS, C, RET_QK_DIM))
    zeta = jnp.broadcast_to(zeta[:, :, None], (RET_HEADS, C, RET_QK_DIM))
    cd = jnp.broadcast_to(chunk_decay[:, None, None], (RET_HEADS, 8, RET_V_DIM))
    return dmask, xi, zeta, cd


def _ret_kernel(h_ref, gain_ref, win_ref, wout_ref, dmask_ref, xi_ref, zeta_ref, cd_ref, state0_ref,
                *rest, n_chunks, emit_state):
    if emit_state:
        o_ref, state_out_ref, state_ref, proj_ref, og_ref = rest
    else:
        o_ref, state_ref, proj_ref, og_ref = rest
    H, dk, dv, C = RET_HEADS, RET_QK_DIM, RET_V_DIM, RET_CHUNK
    t = pl.program_id(1)

    @pl.when(t == 0)
    def _():
        state_ref[...] = state0_ref[...]

    x = h_ref[0]
    proj_ref[...] = _dot(_rms_norm(x, gain_ref[...]).astype(BF16), win_ref[...])
    scale = dk ** -0.5
    for c in range(n_chunks):
        rows = slice(c * C, (c + 1) * C)
        for hh in range(H):
            q = proj_ref[rows, hh * dk:(hh + 1) * dk] * scale
            k = proj_ref[rows, H * dk + hh * dk:H * dk + (hh + 1) * dk]
            v = proj_ref[rows, 2 * H * dk + hh * dv:2 * H * dk + (hh + 1) * dv].astype(BF16)
            gate = proj_ref[rows, 2 * H * dk + H * dv + hh * dv:2 * H * dk + H * dv + (hh + 1) * dv]
            scores = _dot_nt(q.astype(BF16), k.astype(BF16)) * dmask_ref[hh]
            intra = _dot(scores.astype(BF16), v)
            state = state_ref[hh]
            inter = _dot((q * xi_ref[hh]).astype(BF16), state.astype(BF16))
            kz = (k * zeta_ref[hh]).astype(BF16)
            state_ref[hh] = state * cd_ref[hh][0:1, :] + _dot_tn(kz, v)
            o = intra + inter
            mu = jnp.mean(o, axis=-1, keepdims=True)
            oc = o - mu
            var = jnp.mean(oc * oc, axis=-1, keepdims=True)
            on = oc * lax.rsqrt(var + GN_EPS)
            og_ref[rows, hh * dv:(hh + 1) * dv] = (gate * jax.nn.sigmoid(gate) * on).astype(BF16)
    o_ref[0] = x + _dot(og_ref[...], wout_ref[...])

    if emit_state:
        @pl.when(t == pl.num_programs(1) - 1)
        def _():
            state_out_ref[0] = state_ref[...]


def _ret_mixer(h, gain, w_in, w_out, layer, tables, state0, *, tile, emit_state):
    B, L, D = h.shape
    H, dk, dv, C = RET_HEADS, RET_QK_DIM, RET_V_DIM, RET_CHUNK
    dmask, xi, zeta, cd = tables
    out_shape = [jax.ShapeDtypeStruct((B, L, D), F32)]
    out_specs = [pl.BlockSpec((1, tile, D), lambda b, t: (b, t, 0))]
    if emit_state:
        out_shape.append(jax.ShapeDtypeStruct((B, H, dk, dv), F32))
        out_specs.append(pl.BlockSpec((1, H, dk, dv), lambda b, t: (b, 0, 0, 0)))
    res = pl.pallas_call(
        functools.partial(_ret_kernel, n_chunks=tile // C, emit_state=emit_state),
        grid=(B, L // tile),
        in_specs=[
            pl.BlockSpec((1, tile, D), lambda b, t: (b, t, 0)),
            _resident_layer(layer, (1, D)),
            _resident_layer(layer, (D, RET_PROJ)),
            _resident_layer(layer, (H * dv, D)),
            _resident((H, C, C)),
            _resident((H, C, dk)),
            _resident((H, C, dk)),
            _resident((H, 8, dv)),
            _resident((H, dk, dv)),
        ],
        out_specs=out_specs,
        out_shape=out_shape,
        scratch_shapes=[
            pltpu.VMEM((H, dk, dv), F32),
            pltpu.VMEM((tile, RET_PROJ), F32),
            pltpu.VMEM((tile, H * dv), BF16),
        ],
        compiler_params=_params(2),
        name="ret_mixer_meta" if emit_state else "ret_mixer",
    )(h, gain, w_in, w_out, dmask, xi, zeta, cd, state0)
    return res if emit_state else res[0]


def _ffn_body(x, gain_ref, win_ref, wout_ref, act_ref):
    F, CH = FFN_HIDDEN, FFN_COL_CHUNK
    xn = _rms_norm(x, gain_ref[...]).astype(BF16)
    for j in range(F // CH):
        gate = _dot(xn, win_ref[:, j * CH:(j + 1) * CH])
        up = _dot(xn, win_ref[:, F + j * CH:F + (j + 1) * CH])
        act_ref[:, j * CH:(j + 1) * CH] = (gate * jax.nn.sigmoid(gate) * up).astype(BF16)
    return x + _dot(act_ref[...], wout_ref[...])


def _ffn_kernel(h_ref, gain_ref, win_ref, wout_ref, *rest, with_kv):
    if with_kv:
        kv_gain_ref, kv_w_ref, o_ref, kv_ref, act_ref = rest
    else:
        o_ref, act_ref = rest
    y = _ffn_body(h_ref[...], gain_ref, win_ref, wout_ref, act_ref)
    o_ref[...] = y
    if with_kv:
        kv_ref[...] = _dot(_rms_norm(y, kv_gain_ref[...]).astype(BF16), kv_w_ref[...]).astype(BF16)


def _ffn(h, gain, w_in, w_out, layer, *, tile, kv_gain=None, kv_w=None):
    T, D = h.shape
    F = FFN_HIDDEN
    with_kv = kv_w is not None
    in_specs = [
        pl.BlockSpec((tile, D), lambda t: (t, 0)),
        _resident_layer(layer, (1, D)),
        _resident_layer(layer, (D, 2 * F)),
        _resident_layer(layer, (F, D)),
    ]
    args = [h, gain, w_in, w_out]
    out_shape = [jax.ShapeDtypeStruct((T, D), F32)]
    out_specs = [pl.BlockSpec((tile, D), lambda t: (t, 0))]
    if with_kv:
        in_specs += [_resident((1, D)), _resident((D, 2 * SWA_KV_DIM))]
        args += [kv_gain, kv_w]
        out_shape.append(jax.ShapeDtypeStruct((T, 2 * SWA_KV_DIM), BF16))
        out_specs.append(pl.BlockSpec((tile, 2 * SWA_KV_DIM), lambda t: (t, 0)))
    res = pl.pallas_call(
        functools.partial(_ffn_kernel, with_kv=with_kv),
        grid=(T // tile,),
        in_specs=in_specs,
        out_specs=out_specs,
        out_shape=out_shape,
        scratch_shapes=[pltpu.VMEM((tile, F), BF16)],
        compiler_params=_params(1),
        name="ffn_kv" if with_kv else "ffn",
    )(*args)
    return res if with_kv else res[0]


def _swa_bias_tables():
    W = SWA_WINDOW
    slopes = jnp.exp2(-8.0 * (jnp.arange(SWA_HEADS, dtype=F32) + 1.0) / SWA_HEADS)
    r = jnp.arange(W)[:, None]
    c = jnp.arange(W)[None, :]
    d_prev = (W + r - c).astype(F32)
    d_cur = (r - c).astype(F32)
    s = slopes[:, None, None]
    b_prev = jnp.where((c > r)[None], -s * d_prev[None], NEG_INF)
    b_cur = jnp.where((c <= r)[None], -s * d_cur[None], NEG_INF)
    lane = jnp.arange(W)[None, :]
    lane_kv = lane // N_META
    lane_m = lane % N_META
    d_meta = (N_META + r - lane_m).astype(F32)
    head_kv = (jnp.arange(SWA_HEADS) // SWA_GROUP)[:, None, None]
    b_meta = jnp.where(lane_kv[None] == head_kv, -s * d_meta[None], NEG_INF)
    return jnp.stack([b_prev, b_cur, b_meta], axis=1), slopes


def _swa_steps(h_ref, t, sinks_ref, slopes_ref, gain_ref, wq_ref, wo_ref, kvp_ref, kvc_ref, kvm_ref, bias_ref,
               q_ref, km_ref, vm_ref, kmeta_ref, vmeta_ref, sb_ref, sm_ref, pb_ref, pm_ref, att_ref, mid_ref,
               *, n_blocks):
    W, G, KVH, KD = SWA_WINDOW, SWA_GROUP, SWA_KV_HEADS, SWA_KV_DIM
    half = n_blocks * W // 2

    def q_proj(rows):
        xn = _rms_norm(h_ref[rows, :], gain_ref[...]).astype(BF16)
        q_ref[rows, :] = (_dot(xn, wq_ref[...]) * (SWA_HEAD_DIM ** -0.5)).astype(BF16)

    def o_proj(rows):
        mid_ref[rows, :] = h_ref[rows, :] + _dot(att_ref[rows, :], wo_ref[...])

    q_proj(slice(0, half))
    yield False

    col_kv = lax.broadcasted_iota(jnp.int32, (W, KD), 1) // SWA_HEAD_DIM
    col_kv_meta = lax.broadcasted_iota(jnp.int32, (N_META, KD), 1) // SWA_HEAD_DIM
    zero = jnp.zeros((), BF16)
    kmeta_ref[...] = jnp.zeros_like(kmeta_ref)
    vmeta_ref[...] = jnp.zeros_like(vmeta_ref)
    for kv in range(KVH):
        rows = slice(kv * N_META, (kv + 1) * N_META)
        kmeta_ref[rows, :] = jnp.where(col_kv_meta == kv, kvm_ref[:, :KD], zero)
        vmeta_ref[rows, :] = jnp.where(col_kv_meta == kv, kvm_ref[:, KD:], zero)
    for j in range(n_blocks + 1):
        blk = kvp_ref[...] if j == 0 else kvc_ref[(j - 1) * W:j * W, :]
        for kv in range(KVH):
            rows = slice(j * KVH * W + kv * W, j * KVH * W + (kv + 1) * W)
            km_ref[rows, :] = jnp.where(col_kv == kv, blk[:, :KD], zero)
            vm_ref[rows, :] = jnp.where(col_kv == kv, blk[:, KD:], zero)
    yield False

    first_pen = jnp.where(t == 0, NEG_INF, 0.0).astype(F32)
    for b in range(n_blocks):
        if b * W == half:
            q_proj(slice(half, 2 * half))
            yield False
        qs = jnp.concatenate([q_ref[b * W:(b + 1) * W, g * KD:(g + 1) * KD] for g in range(G)], axis=0)
        key_rows = slice(b * KVH * W, (b + 2) * KVH * W)
        sb_ref[b % 2] = _dot_nt(qs, km_ref[key_rows, :])
        sm_ref[b % 2] = _dot_nt(qs, kmeta_ref[...])
        yield False
        block_index = (t * n_blocks + b).astype(F32)
        for kv in range(KVH):
            for g in range(G):
                head = kv * G + g
                rows = slice(g * W, (g + 1) * W)
                s_prev = sb_ref[b % 2, rows, kv * W:(kv + 1) * W] + bias_ref[head, 0]
                if b == 0:
                    s_prev = s_prev + first_pen
                s_cur = sb_ref[b % 2, rows, KVH * W + kv * W:KVH * W + (kv + 1) * W] + bias_ref[head, 1]
                meta_shift = -(slopes_ref[head] * float(W)) * block_index
                s_meta = sm_ref[b % 2, rows, :] + (bias_ref[head, 2] + meta_shift)
                sink = sinks_ref[head]
                m = jnp.max(jnp.maximum(jnp.maximum(s_prev, s_cur), s_meta), axis=-1, keepdims=True)
                m = jnp.maximum(m, sink)
                e_prev = jnp.exp(s_prev - m)
                e_cur = jnp.exp(s_cur - m)
                e_meta = jnp.exp(s_meta - m)
                denom = jnp.sum(e_prev + e_cur + e_meta, axis=-1, keepdims=True) + jnp.exp(sink - m)
                inv = 1.0 / denom
                pb_ref[b % 2, rows, kv * W:(kv + 1) * W] = (e_prev * inv).astype(BF16)
                pb_ref[b % 2, rows, KVH * W + kv * W:KVH * W + (kv + 1) * W] = (e_cur * inv).astype(BF16)
                pm_ref[b % 2, rows, :] = (e_meta * inv).astype(BF16)
            yield True
        out = _dot(pb_ref[b % 2], vm_ref[key_rows, :]) + _dot(pm_ref[b % 2], vmeta_ref[...])
        for g in range(G):
            att_ref[b * W:(b + 1) * W, g * KD:(g + 1) * KD] = out[g * W:(g + 1) * W, :].astype(BF16)
        yield False
        if (b + 1) * W == half:
            o_proj(slice(0, half))
            yield False
    o_proj(slice(half, 2 * half))
    yield False


def _ffn_steps(x_ref, gain_ref, win_ref, wout_ref, act_ref, o_ref):
    F, CH = FFN_HIDDEN, FFN_COL_CHUNK
    n_chunks = F // CH
    xn = _rms_norm(x_ref[...], gain_ref[...]).astype(BF16)
    first = True
    for j in range(n_chunks):
        gate = _dot(xn, win_ref[:, j * CH:(j + 1) * CH])
        up = _dot(xn, win_ref[:, F + j * CH:F + (j + 1) * CH])
        act_ref[:, j * CH:(j + 1) * CH] = (gate * jax.nn.sigmoid(gate) * up).astype(BF16)
        yield
        if (j + 1) % FFN_OUT_GROUP == 0 or j == n_chunks - 1:
            lo = (j // FFN_OUT_GROUP) * FFN_OUT_GROUP * CH
            part = _dot(act_ref[:, lo:(j + 1) * CH], wout_ref[lo:(j + 1) * CH, :])
            o_ref[...] = (x_ref[...] if first else o_ref[...]) + part
            first = False
            yield


def _swa_layer_kernel(sinks_ref, slopes_ref, h_ref, gain_ref, wq_ref, wo_ref, kvp_ref, kvc_ref, kvm_ref,
                      bias_ref, fgain_ref, fwin_ref, fwout_ref, *rest, n_blocks, tiles_per_seq, final_norm):
    if final_norm:
        final_gain_ref, o_ref, *scratch = rest
    else:
        o_ref, *scratch = rest
    mid_ref, act_ref, mid_next_ref = scratch[-3:]
    i = pl.program_id(0)

    @pl.when(i == 0)
    def _():
        mid_ref[...] = jnp.zeros_like(mid_ref)

    t = lax.rem(jnp.minimum(i, pl.num_programs(0) - 2), tiles_per_seq)
    ffn = _ffn_steps(mid_ref, fgain_ref, fwin_ref, fwout_ref, act_ref, o_ref)
    mixer = _swa_steps(h_ref, t, sinks_ref, slopes_ref, gain_ref, wq_ref, wo_ref, kvp_ref, kvc_ref, kvm_ref,
                       bias_ref, *scratch[:-3], mid_next_ref, n_blocks=n_blocks)
    next(ffn, None)
    for was_softmax in mixer:
        if was_softmax:
            next(ffn, None)
    for _ in ffn:
        pass
    if final_norm:
        o_ref[...] = _rms_norm(o_ref[...], final_gain_ref[...])
    mid_ref[...] = mid_next_ref[...]


def _swa_layer(h, mix_gain, wq, wo, sinks, slopes, kv, kv_meta, bias, ffn_gain, f_in, f_out, layer, swa_index,
               seq_len, *, tile, final_gain=None):
    T, D = h.shape
    W, G, KVH, KD, F = SWA_WINDOW, SWA_GROUP, SWA_KV_HEADS, SWA_KV_DIM, FFN_HIDDEN
    n_blocks = tile // W
    n_tiles = T // tile
    final_norm = final_gain is not None
    smem = pl.BlockSpec(memory_space=pltpu.SMEM)
    cur = lambda i: jnp.minimum(i, n_tiles - 1)
    in_specs = [
        smem, smem,
        pl.BlockSpec((tile, D), lambda i: (cur(i), 0)),
        _resident_layer(layer, (1, D)),
        _resident_layer(swa_index, (D, SWA_Q_DIM)),
        _resident_layer(swa_index, (SWA_Q_DIM, D)),
        pl.BlockSpec((W, 2 * KD), lambda i: (jnp.maximum(cur(i) * n_blocks - 1, 0), 0)),
        pl.BlockSpec((tile, 2 * KD), lambda i: (cur(i), 0)),
        _resident((N_META, 2 * KD)),
        _resident((SWA_HEADS, 3, W, W)),
        _resident_layer(layer, (1, D)),
        _resident_layer(layer, (D, 2 * F)),
        _resident_layer(layer, (F, D)),
    ]
    args = [sinks, slopes, h, mix_gain, wq, wo, kv, kv, kv_meta, bias, ffn_gain, f_in, f_out]
    if final_norm:
        in_specs.append(_resident((1, D)))
        args.append(final_gain)
    return pl.pallas_call(
        functools.partial(_swa_layer_kernel, n_blocks=n_blocks, tiles_per_seq=seq_len // tile,
                          final_norm=final_norm),
        grid=(n_tiles + 1,),
        in_specs=in_specs,
        out_specs=pl.BlockSpec((tile, D), lambda i: (jnp.maximum(i - 1, 0), 0)),
        out_shape=jax.ShapeDtypeStruct((T, D), F32),
        scratch_shapes=[
            pltpu.VMEM((tile, SWA_Q_DIM), BF16),
            pltpu.VMEM(((n_blocks + 1) * KVH * W, KD), BF16),
            pltpu.VMEM(((n_blocks + 1) * KVH * W, KD), BF16),
            pltpu.VMEM((W, KD), BF16),
            pltpu.VMEM((W, KD), BF16),
            pltpu.VMEM((2, G * W, 2 * KVH * W), F32),
            pltpu.VMEM((2, G * W, W), F32),
            pltpu.VMEM((2, G * W, 2 * KVH * W), BF16),
            pltpu.VMEM((2, G * W, W), BF16),
            pltpu.VMEM((tile, SWA_Q_DIM), BF16),
            pltpu.VMEM((tile, D), F32),
            pltpu.VMEM((tile, F), BF16),
            pltpu.VMEM((tile, D), F32),
        ],
        compiler_params=_params(1),
        name="swa_layer_final" if final_norm else "swa_layer",
    )(*args)


def _group_major(w, axis):
    shape = w.shape
    split = shape[:axis] + (SWA_KV_HEADS, SWA_GROUP, SWA_HEAD_DIM) + shape[axis + 1:]
    return jnp.swapaxes(w.reshape(split), axis, axis + 1).reshape(shape)


def kernel(x, meta_tokens, mix_norm, ffn_norm, ret_w_in, ret_w_out, kv_norm, kv_w, swa_w_q, swa_w_o,
           swa_sinks, ffn_w_in, ffn_w_out, final_norm):
    B, S, D = x.shape
    n_ret = ret_w_in.shape[0]
    n_swa = swa_w_q.shape[0]
    depth = n_ret + n_swa
    assert D == D_MODEL and S % TOKEN_TILE == 0

    tables = _ret_tables()
    bias, slopes = _swa_bias_tables()
    mix_gain = mix_norm.astype(F32).reshape(depth, 1, D)
    ffn_gain = ffn_norm.astype(F32).reshape(depth, 1, D)
    ret_in16, ret_out16 = ret_w_in.astype(BF16), ret_w_out.astype(BF16)
    ffn_in16, ffn_out16 = ffn_w_in.astype(BF16), ffn_w_out.astype(BF16)
    wq16 = _group_major(swa_w_q, 2).astype(BF16)
    wo16 = _group_major(swa_w_o, 1).astype(BF16)
    kv_w16 = kv_w.astype(BF16)
    kv_gain = kv_norm.astype(F32).reshape(1, D)

    hm = jnp.concatenate([jnp.zeros((META_TILE - N_META, D), F32), meta_tokens.astype(F32)], axis=0)
    h = x.astype(F32)
    state0 = jnp.zeros((RET_HEADS, RET_QK_DIM, RET_V_DIM), F32)
    kv = kv_meta = None

    for layer in range(n_ret):
        hm, state_meta = _ret_mixer(hm[None], mix_gain, ret_in16, ret_out16, layer, tables, state0,
                                    tile=META_TILE, emit_state=True)
        h = _ret_mixer(h, mix_gain, ret_in16, ret_out16, layer, tables, state_meta[0],
                       tile=TOKEN_TILE, emit_state=False)
        if layer == n_ret - 1:
            hm, kv_meta = _ffn(hm[0], ffn_gain, ffn_in16, ffn_out16, layer, tile=META_TILE,
                               kv_gain=kv_gain, kv_w=kv_w16)
            h, kv = _ffn(h.reshape(B * S, D), ffn_gain, ffn_in16, ffn_out16, layer, tile=TOKEN_TILE,
                         kv_gain=kv_gain, kv_w=kv_w16)
        else:
            hm = _ffn(hm[0], ffn_gain, ffn_in16, ffn_out16, layer, tile=META_TILE)
            h = _ffn(h.reshape(B * S, D), ffn_gain, ffn_in16, ffn_out16, layer, tile=TOKEN_TILE).reshape(B, S, D)

    kv_meta = kv_meta[META_TILE - N_META:]
    for b in range(n_swa):
        layer = n_ret + b
        final_gain = final_norm.astype(F32).reshape(1, D) if layer == depth - 1 else None
        h = _swa_layer(h, mix_gain, wq16, wo16, swa_sinks[b].astype(F32), slopes, kv, kv_meta, bias,
                       ffn_gain, ffn_in16, ffn_out16, layer, b, S, tile=TOKEN_TILE, final_gain=final_gain)
    return h.reshape(B, S, D)
```

```python
import functools

import jax
import jax.numpy as jnp
from jax import lax
from jax.experimental import pallas as pl
from jax.experimental.pallas import tpu as pltpu

D_MODEL = 1024
N_META = 16
RET_HEADS = 4
RET_QK_DIM = 256
RET_V_DIM = 512
RET_CHUNK = 128
RET_PROJ = 2 * RET_HEADS * RET_QK_DIM + 2 * RET_HEADS * RET_V_DIM
SWA_HEADS = 16
SWA_KV_HEADS = 4
SWA_GROUP = 4
SWA_HEAD_DIM = 64
SWA_WINDOW = 128
SWA_Q_DIM = SWA_HEADS * SWA_HEAD_DIM
SWA_KV_DIM = SWA_KV_HEADS * SWA_HEAD_DIM
FFN_HIDDEN = 2816
FFN_COL_CHUNK = 256
FFN_OUT_GROUP = 4
RMS_EPS = 1e-6
GN_EPS = 1e-6

TOKEN_TILE = 512
FFN_TILE = 1024
META_TILE = RET_CHUNK
RET_MAIN_CHUNK = 256
VMEM_LIMIT_BYTES = 56 * 1024 * 1024

BF16 = jnp.bfloat16
F32 = jnp.float32
NEG_INF = float("-inf")


def _resident(block_shape):
    zeros = (0,) * len(block_shape)
    return pl.BlockSpec(block_shape, lambda *_: zeros, pipeline_mode=pl.Buffered(1))


def _resident_layer(layer, tail_shape):
    index = (layer,) + (0,) * len(tail_shape)
    return pl.BlockSpec((None,) + tuple(tail_shape), lambda *_: index, pipeline_mode=pl.Buffered(1))


def _rms_norm(x, gain):
    ms = jnp.mean(x * x, axis=-1, keepdims=True)
    return x * lax.rsqrt(ms + RMS_EPS) * gain


def _dot(a, b):
    return jnp.dot(a, b, preferred_element_type=F32)


def _dot_nt(a, b):
    return lax.dot_general(a, b, (((1,), (1,)), ((), ())), preferred_element_type=F32)


def _dot_tn(a, b):
    return lax.dot_general(a, b, (((0,), (0,)), ((), ())), preferred_element_type=F32)


def _params(n_grid_axes, flags=None):
    return pltpu.CompilerParams(dimension_semantics=("arbitrary",) * n_grid_axes,
                                vmem_limit_bytes=VMEM_LIMIT_BYTES, flags=flags)


def _ret_tables(C):
    log_gamma = jnp.log1p(-jnp.exp2(-5.0 - jnp.arange(RET_HEADS, dtype=F32)))
    i = jnp.arange(C, dtype=F32)
    diff = i[:, None] - i[None, :]
    dmask = jnp.where(diff >= 0, jnp.exp(log_gamma[:, None, None] * jnp.maximum(diff, 0.0)), 0.0)
    zeta = jnp.exp(log_gamma[:, None] * (C - 1.0 - i)[None, :])
    xi = jnp.exp(log_gamma[:, None] * (i + 1.0)[None, :])
    chunk_decay = jnp.exp(log_gamma * C)
    xi = jnp.broadcast_to(xi[:, :, None], (RET_HEADS, C, RET_QK_DIM))
    zeta = jnp.broadcast_to(zeta[:, :, None], (RET_HEADS, C, RET_QK_DIM))
    cd = jnp.broadcast_to(chunk_decay[:, None, None], (RET_HEADS, 8, RET_V_DIM))
    return dmask, xi, zeta, cd


def _ret_kernel(h_ref, gain_ref, win_ref, wout_ref, dmask_ref, xi_ref, zeta_ref, cd_ref, state0_ref,
                *rest, chunk, n_chunks, emit_state):
    if emit_state:
        o_ref, state_out_ref, state_ref, proj_ref, og_ref = rest
    else:
        o_ref, state_ref, proj_ref, og_ref = rest
    H, dk, dv, C = RET_HEADS, RET_QK_DIM, RET_V_DIM, chunk
    t = pl.program_id(1)

    @pl.when(t == 0)
    def _():
        state_ref[...] = state0_ref[...]

    x = h_ref[0]
    proj_ref[...] = _dot(_rms_norm(x, gain_ref[...]).astype(BF16), win_ref[...])
    scale = dk ** -0.5
    for c in range(n_chunks):
        rows = slice(c * C, (c + 1) * C)
        for hh in range(H):
            q = proj_ref[rows, hh * dk:(hh + 1) * dk] * scale
            k = proj_ref[rows, H * dk + hh * dk:H * dk + (hh + 1) * dk]
            v = proj_ref[rows, 2 * H * dk + hh * dv:2 * H * dk + (hh + 1) * dv].astype(BF16)
            gate = proj_ref[rows, 2 * H * dk + H * dv + hh * dv:2 * H * dk + H * dv + (hh + 1) * dv]
            scores = _dot_nt(q.astype(BF16), k.astype(BF16)) * dmask_ref[hh]
            intra = _dot(scores.astype(BF16), v)
            state = state_ref[hh]
            inter = _dot((q * xi_ref[hh]).astype(BF16), state.astype(BF16))
            kz = (k * zeta_ref[hh]).astype(BF16)
            state_ref[hh] = state * cd_ref[hh][0:1, :] + _dot_tn(kz, v)
            o = intra + inter
            mu = jnp.mean(o, axis=-1, keepdims=True)
            oc = o - mu
            var = jnp.mean(oc * oc, axis=-1, keepdims=True)
            on = oc * lax.rsqrt(var + GN_EPS)
            og_ref[rows, hh * dv:(hh + 1) * dv] = (gate * jax.nn.sigmoid(gate) * on).astype(BF16)
    o_ref[0] = x + _dot(og_ref[...], wout_ref[...])

    if emit_state:
        @pl.when(t == pl.num_programs(1) - 1)
        def _():
            state_out_ref[0] = state_ref[...]


def _ret_mixer(h, gain, w_in, w_out, layer, tables, state0, *, tile, emit_state):
    B, L, D = h.shape
    H, dk, dv = RET_HEADS, RET_QK_DIM, RET_V_DIM
    dmask, xi, zeta, cd = tables
    C = dmask.shape[-1]
    out_shape = [jax.ShapeDtypeStruct((B, L, D), F32)]
    out_specs = [pl.BlockSpec((1, tile, D), lambda b, t: (b, t, 0))]
    if emit_state:
        out_shape.append(jax.ShapeDtypeStruct((B, H, dk, dv), F32))
        out_specs.append(pl.BlockSpec((1, H, dk, dv), lambda b, t: (b, 0, 0, 0)))
    res = pl.pallas_call(
        functools.partial(_ret_kernel, chunk=C, n_chunks=tile // C, emit_state=emit_state),
        grid=(B, L // tile),
        in_specs=[
            pl.BlockSpec((1, tile, D), lambda b, t: (b, t, 0)),
            _resident_layer(layer, (1, D)),
            _resident_layer(layer, (D, RET_PROJ)),
            _resident_layer(layer, (H * dv, D)),
            _resident((H, C, C)),
            _resident((H, C, dk)),
            _resident((H, C, dk)),
            _resident((H, 8, dv)),
            _resident((H, dk, dv)),
        ],
        out_specs=out_specs,
        out_shape=out_shape,
        scratch_shapes=[
            pltpu.VMEM((H, dk, dv), F32),
            pltpu.VMEM((tile, RET_PROJ), F32),
            pltpu.VMEM((tile, H * dv), BF16),
        ],
        compiler_params=_params(2),
        name="ret_mixer_meta" if emit_state else "ret_mixer",
    )(h, gain, w_in, w_out, dmask, xi, zeta, cd, state0)
    return res if emit_state else res[0]


def _ffn_body(x, gain_ref, win_ref, wout_ref, act_ref):
    F, CH = FFN_HIDDEN, FFN_COL_CHUNK
    xn = _rms_norm(x, gain_ref[...]).astype(BF16)
    for j in range(F // CH):
        gate = _dot(xn, win_ref[:, j * CH:(j + 1) * CH])
        up = _dot(xn, win_ref[:, F + j * CH:F + (j + 1) * CH])
        act_ref[:, j * CH:(j + 1) * CH] = (gate * jax.nn.sigmoid(gate) * up).astype(BF16)
    return x + _dot(act_ref[...], wout_ref[...])


def _ffn_kernel(h_ref, gain_ref, win_ref, wout_ref, *rest, with_kv):
    if with_kv:
        kv_gain_ref, wk_ref, wvt_ref, o_ref, k_ref, vt_ref, act_ref = rest
    else:
        o_ref, act_ref = rest
    y = _ffn_body(h_ref[...], gain_ref, win_ref, wout_ref, act_ref)
    o_ref[...] = y
    if with_kv:
        xn = _rms_norm(y, kv_gain_ref[...]).astype(BF16)
        k_ref[...] = _dot(xn, wk_ref[...]).astype(BF16)
        vt_ref[...] = _dot_nt(wvt_ref[...], xn).astype(BF16)


def _ffn(h, gain, w_in, w_out, layer, *, tile, kv_gain=None, wk=None, wvt=None):
    T, D = h.shape
    F, KD = FFN_HIDDEN, SWA_KV_DIM
    with_kv = wk is not None
    in_specs = [
        pl.BlockSpec((tile, D), lambda t: (t, 0)),
        _resident_layer(layer, (1, D)),
        _resident_layer(layer, (D, 2 * F)),
        _resident_layer(layer, (F, D)),
    ]
    args = [h, gain, w_in, w_out]
    out_shape = [jax.ShapeDtypeStruct((T, D), F32)]
    out_specs = [pl.BlockSpec((tile, D), lambda t: (t, 0))]
    if with_kv:
        in_specs += [_resident((1, D)), _resident((D, KD)), _resident((KD, D))]
        args += [kv_gain, wk, wvt]
        out_shape += [jax.ShapeDtypeStruct((T, KD), BF16), jax.ShapeDtypeStruct((KD, T), BF16)]
        out_specs += [pl.BlockSpec((tile, KD), lambda t: (t, 0)), pl.BlockSpec((KD, tile), lambda t: (0, t))]
    res = pl.pallas_call(
        functools.partial(_ffn_kernel, with_kv=with_kv),
        grid=(T // tile,),
        in_specs=in_specs,
        out_specs=out_specs,
        out_shape=out_shape,
        scratch_shapes=[pltpu.VMEM((tile, F), BF16)],
        compiler_params=_params(1),
        name="ffn_kv" if with_kv else "ffn",
    )(*args)
    return res if with_kv else res[0]


def _swa_bias_tables():
    W = SWA_WINDOW
    slopes = jnp.exp2(-8.0 * (jnp.arange(SWA_HEADS, dtype=F32) + 1.0) / SWA_HEADS)
    c = jnp.arange(W)[:, None]
    r = jnp.arange(W)[None, :]
    s = slopes[:, None, None]
    b_prev = jnp.where((c > r)[None], -s * (W + r - c).astype(F32)[None], NEG_INF)
    b_cur = jnp.where((c <= r)[None], -s * (r - c).astype(F32)[None], NEG_INF)
    m = jnp.arange(N_META)[:, None]
    b_meta = -s * (N_META + r - m).astype(F32)[None]
    return jnp.concatenate([b_prev, b_meta, b_cur], axis=1), slopes


def _swa_steps(h_ref, t, sinks_ref, slopes_ref, gain_ref, wq_ref, wo_ref, kp_ref, kc_ref, vtp_ref, vtc_ref,
               kmeta_ref, vtmeta_ref, bias_ref, q_ref, km_ref, vmt_ref, st_ref, pt_ref, scale_ref, attt_ref,
               mid_ref, *, n_blocks):
    W, G, KVH, KD, HD = SWA_WINDOW, SWA_GROUP, SWA_KV_HEADS, SWA_KV_DIM, SWA_HEAD_DIM
    BLK = KVH * W
    MET = W
    STRIDE = BLK + MET
    WIN = 2 * BLK + MET
    half_blocks = n_blocks // 2

    def q_proj(first_block):
        rows = slice(first_block * W, (first_block + half_blocks) * W)
        xn = _rms_norm(h_ref[rows, :], gain_ref[...]).astype(BF16)
        for g in range(G):
            qg = (_dot(xn, wq_ref[:, g * KD:(g + 1) * KD]) * (HD ** -0.5)).astype(BF16)
            for bb in range(half_blocks):
                q_ref[first_block + bb, g * W:(g + 1) * W, :] = qg[bb * W:(bb + 1) * W, :]

    def o_proj(first_block):
        rows = slice(first_block * W, (first_block + half_blocks) * W)
        mid_ref[rows, :] = h_ref[rows, :] + _dot_tn(attt_ref[:, rows], wo_ref[...])

    q_proj(0)
    yield False

    col_kv = lax.broadcasted_iota(jnp.int32, (W, KD), 1) // HD
    col_kv_meta = lax.broadcasted_iota(jnp.int32, (N_META, KD), 1) // HD
    row_kv = lax.broadcasted_iota(jnp.int32, (KD, W), 0) // HD
    lane = lax.broadcasted_iota(jnp.int32, (KD, W), 1)
    zero = jnp.zeros((), BF16)
    kmeta_bd = jnp.concatenate(
        [jnp.where(col_kv_meta == kv, kmeta_ref[...], zero) for kv in range(KVH)]
        + [jnp.zeros((MET - KVH * N_META, KD), BF16)], axis=0)
    vtmeta_bd = jnp.where((lane // N_META == row_kv) & (lane < KVH * N_META), vtmeta_ref[...], zero)
    for j in range(n_blocks + 1):
        kblk = kp_ref[...] if j == 0 else kc_ref[(j - 1) * W:j * W, :]
        vblk = vtp_ref[...] if j == 0 else vtc_ref[:, (j - 1) * W:j * W]
        for kv in range(KVH):
            span = slice(j * STRIDE + kv * W, j * STRIDE + (kv + 1) * W)
            km_ref[span, :] = jnp.where(col_kv == kv, kblk, zero)
            vmt_ref[:, span] = jnp.where(row_kv == kv, vblk, zero)
        if j < n_blocks:
            span = slice(j * STRIDE + BLK, (j + 1) * STRIDE)
            km_ref[span, :] = kmeta_bd
            vmt_ref[:, span] = vtmeta_bd
    yield False

    first_pen = jnp.where(t == 0, NEG_INF, 0.0).astype(F32)
    for b in range(n_blocks):
        if b == half_blocks:
            q_proj(half_blocks)
            yield False
        slot = b % 2
        window = slice(b * STRIDE, b * STRIDE + WIN)
        st_ref[slot, :, :G * W] = _dot_nt(km_ref[window, :], q_ref[b])
        pt_ref[slot, BLK + KVH * N_META:STRIDE, :G * W] = jnp.zeros((MET - KVH * N_META, G * W), BF16)
        yield False
        block_index = (t * n_blocks + b).astype(F32)
        for kv in range(KVH):
            prev_rows = slice(kv * W, (kv + 1) * W)
            meta_rows = slice(BLK + kv * N_META, BLK + (kv + 1) * N_META)
            cur_rows = slice(STRIDE + kv * W, STRIDE + (kv + 1) * W)
            for g in range(G):
                head = kv * G + g
                cols = slice(g * W, (g + 1) * W)
                sink = sinks_ref[head]
                meta_shift = -(slopes_ref[head] * float(W)) * block_index
                s_prev = st_ref[slot, prev_rows, cols] + bias_ref[head, 0:W, :]
                if b == 0:
                    s_prev = s_prev + first_pen
                s_meta = st_ref[slot, meta_rows, cols] + (bias_ref[head, W:W + N_META, :] + meta_shift)
                s_cur = st_ref[slot, cur_rows, cols] + bias_ref[head, W + N_META:, :]
                m = jnp.maximum(jnp.maximum(jnp.max(s_prev, axis=0, keepdims=True),
                                            jnp.max(s_cur, axis=0, keepdims=True)),
                                jnp.max(s_meta, axis=0, keepdims=True))
                m = jnp.maximum(m, sink)
                e_prev = jnp.exp(s_prev - m)
                e_meta = jnp.exp(s_meta - m)
                e_cur = jnp.exp(s_cur - m)
                denom = (jnp.sum(e_prev, axis=0, keepdims=True) + jnp.sum(e_cur, axis=0, keepdims=True)
                         + jnp.sum(e_meta, axis=0, keepdims=True) + jnp.exp(sink - m))
                pt_ref[slot, prev_rows, cols] = e_prev.astype(BF16)
                pt_ref[slot, meta_rows, cols] = e_meta.astype(BF16)
                pt_ref[slot, cur_rows, cols] = e_cur.astype(BF16)
                scale_ref[slot, kv * HD:(kv + 1) * HD, cols] = jnp.broadcast_to(1.0 / denom, (HD, W))
            yield True
        out_t = _dot(vmt_ref[:, window], pt_ref[slot, :, :G * W]) * scale_ref[slot, :, :G * W]
        for g in range(G):
            attt_ref[g * KD:(g + 1) * KD, b * W:(b + 1) * W] = out_t[:, g * W:(g + 1) * W].astype(BF16)
        yield False
        if b + 1 == half_blocks:
            o_proj(0)
            yield False
    o_proj(half_blocks)
    yield False


def _ffn_steps(x_ref, gain_ref, win_ref, wout_ref, act_ref, o_ref):
    F, CH = FFN_HIDDEN, FFN_COL_CHUNK
    n_chunks = F // CH
    xn = _rms_norm(x_ref[...], gain_ref[...]).astype(BF16)
    first = True
    for j in range(n_chunks):
        gate = _dot(xn, win_ref[:, j * CH:(j + 1) * CH])
        up = _dot(xn, win_ref[:, F + j * CH:F + (j + 1) * CH])
        act_ref[:, j * CH:(j + 1) * CH] = (gate * jax.nn.sigmoid(gate) * up).astype(BF16)
        yield
        if (j + 1) % FFN_OUT_GROUP == 0 or j == n_chunks - 1:
            lo = (j // FFN_OUT_GROUP) * FFN_OUT_GROUP * CH
            for n in range(D_MODEL // CH):
                cols = slice(n * CH, (n + 1) * CH)
                part = _dot(act_ref[:, lo:(j + 1) * CH], wout_ref[lo:(j + 1) * CH, cols])
                o_ref[:, cols] = (x_ref[:, cols] if first else o_ref[:, cols]) + part
            first = False
            yield


def _swa_layer_kernel(sinks_ref, slopes_ref, h_ref, gain_ref, wq_ref, wo_ref, kp_ref, kc_ref, vtp_ref, vtc_ref,
                      kmeta_ref, vtmeta_ref, bias_ref, fgain_ref, fwin_ref, fwout_ref, *rest,
                      n_blocks, tiles_per_seq, final_norm):
    if final_norm:
        final_gain_ref, o_ref, *scratch = rest
    else:
        o_ref, *scratch = rest
    mid_ref, act_ref = scratch[-2:]
    i = pl.program_id(0)

    @pl.when(i == 0)
    def _():
        mid_ref[1] = jnp.zeros((mid_ref.shape[1], mid_ref.shape[2]), F32)

    t = lax.rem(jnp.minimum(i, pl.num_programs(0) - 2), tiles_per_seq)
    slot = lax.rem(i, 2)
    ffn = _ffn_steps(mid_ref.at[1 - slot], fgain_ref, fwin_ref, fwout_ref, act_ref, o_ref)
    mixer = _swa_steps(h_ref, t, sinks_ref, slopes_ref, gain_ref, wq_ref, wo_ref, kp_ref, kc_ref, vtp_ref, vtc_ref,
                       kmeta_ref, vtmeta_ref, bias_ref, *scratch[:-2], mid_ref.at[slot], n_blocks=n_blocks)
    next(ffn, None)
    for was_softmax in mixer:
        if was_softmax:
            next(ffn, None)
    for _ in ffn:
        pass
    if final_norm:
        o_ref[...] = _rms_norm(o_ref[...], final_gain_ref[...])


def _swa_layer(h, mix_gain, wq, wo, sinks, slopes, k, vt, k_meta, vt_meta, bias, ffn_gain, f_in, f_out, layer,
               swa_index, seq_len, *, tile, final_gain=None):
    T, D = h.shape
    W, G, KVH, KD, F = SWA_WINDOW, SWA_GROUP, SWA_KV_HEADS, SWA_KV_DIM, FFN_HIDDEN
    n_blocks = tile // W
    n_tiles = T // tile
    stride = (KVH + 1) * W
    final_norm = final_gain is not None
    smem = pl.BlockSpec(memory_space=pltpu.SMEM)
    cur = lambda i: jnp.minimum(i, n_tiles - 1)
    prev_block = lambda i: jnp.maximum(cur(i) * n_blocks - 1, 0)
    in_specs = [
        smem, smem,
        pl.BlockSpec((tile, D), lambda i: (cur(i), 0)),
        _resident_layer(layer, (1, D)),
        _resident_layer(swa_index, (D, SWA_Q_DIM)),
        _resident_layer(swa_index, (SWA_Q_DIM, D)),
        pl.BlockSpec((W, KD), lambda i: (prev_block(i), 0)),
        pl.BlockSpec((tile, KD), lambda i: (cur(i), 0)),
        pl.BlockSpec((KD, W), lambda i: (0, prev_block(i))),
        pl.BlockSpec((KD, tile), lambda i: (0, cur(i))),
        _resident((N_META, KD)),
        _resident((KD, W)),
        _resident((SWA_HEADS, 2 * W + N_META, W)),
        _resident_layer(layer, (1, D)),
        _resident_layer(layer, (D, 2 * F)),
        _resident_layer(layer, (F, D)),
    ]
    args = [sinks, slopes, h, mix_gain, wq, wo, k, k, vt, vt, k_meta, vt_meta, bias, ffn_gain, f_in, f_out]
    if final_norm:
        in_specs.append(_resident((1, D)))
        args.append(final_gain)
    return pl.pallas_call(
        functools.partial(_swa_layer_kernel, n_blocks=n_blocks, tiles_per_seq=seq_len // tile,
                          final_norm=final_norm),
        grid=(n_tiles + 1,),
        in_specs=in_specs,
        out_specs=pl.BlockSpec((tile, D), lambda i: (jnp.maximum(i - 1, 0), 0)),
        out_shape=jax.ShapeDtypeStruct((T, D), F32),
        scratch_shapes=[
            pltpu.VMEM((n_blocks, G * W, KD), BF16),
            pltpu.VMEM((n_blocks * stride + KVH * W, KD), BF16),
            pltpu.VMEM((KD, n_blocks * stride + KVH * W + W), BF16),
            pltpu.VMEM((2, stride + KVH * W, G * W + W), F32),
            pltpu.VMEM((2, stride + KVH * W, G * W + W), BF16),
            pltpu.VMEM((2, KD, G * W + W), F32),
            pltpu.VMEM((SWA_Q_DIM, tile + W), BF16),
            pltpu.VMEM((2, tile, D), F32),
            pltpu.VMEM((tile, F), BF16),
        ],
        compiler_params=_params(1),
        name="swa_layer_final" if final_norm else "swa_layer",
    )(*args)


def _group_major(w, axis):
    shape = w.shape
    split = shape[:axis] + (SWA_KV_HEADS, SWA_GROUP, SWA_HEAD_DIM) + shape[axis + 1:]
    return jnp.swapaxes(w.reshape(split), axis, axis + 1).reshape(shape)


def kernel(x, meta_tokens, mix_norm, ffn_norm, ret_w_in, ret_w_out, kv_norm, kv_w, swa_w_q, swa_w_o,
           swa_sinks, ffn_w_in, ffn_w_out, final_norm):
    B, S, D = x.shape
    n_ret = ret_w_in.shape[0]
    n_swa = swa_w_q.shape[0]
    depth = n_ret + n_swa
    assert D == D_MODEL and S % TOKEN_TILE == 0

    meta_tables = _ret_tables(META_TILE)
    tables = _ret_tables(RET_MAIN_CHUNK)
    bias, slopes = _swa_bias_tables()
    mix_gain = mix_norm.astype(F32).reshape(depth, 1, D)
    ffn_gain = ffn_norm.astype(F32).reshape(depth, 1, D)
    ret_in16, ret_out16 = ret_w_in.astype(BF16), ret_w_out.astype(BF16)
    ffn_in16, ffn_out16 = ffn_w_in.astype(BF16), ffn_w_out.astype(BF16)
    wq16 = _group_major(swa_w_q, 2).astype(BF16)
    wo16 = _group_major(swa_w_o, 1).astype(BF16)
    wk16 = kv_w[:, :SWA_KV_DIM].astype(BF16)
    wvt16 = kv_w[:, SWA_KV_DIM:].T.astype(BF16)
    kv_gain = kv_norm.astype(F32).reshape(1, D)

    hm = jnp.concatenate([jnp.zeros((META_TILE - N_META, D), F32), meta_tokens.astype(F32)], axis=0)
    h = x.astype(F32)
    state0 = jnp.zeros((RET_HEADS, RET_QK_DIM, RET_V_DIM), F32)
    k = vt = k_meta = vt_meta = None

    for layer in range(n_ret):
        hm, state_meta = _ret_mixer(hm[None], mix_gain, ret_in16, ret_out16, layer, meta_tables, state0,
                                    tile=META_TILE, emit_state=True)
        h = _ret_mixer(h, mix_gain, ret_in16, ret_out16, layer, tables, state_meta[0],
                       tile=TOKEN_TILE, emit_state=False)
        if layer == n_ret - 1:
            hm, k_meta, vt_meta = _ffn(hm[0], ffn_gain, ffn_in16, ffn_out16, layer, tile=META_TILE,
                                       kv_gain=kv_gain, wk=wk16, wvt=wvt16)
            h, k, vt = _ffn(h.reshape(B * S, D), ffn_gain, ffn_in16, ffn_out16, layer, tile=FFN_TILE,
                            kv_gain=kv_gain, wk=wk16, wvt=wvt16)
        else:
            hm = _ffn(hm[0], ffn_gain, ffn_in16, ffn_out16, layer, tile=META_TILE)
            h = _ffn(h.reshape(B * S, D), ffn_gain, ffn_in16, ffn_out16, layer, tile=FFN_TILE).reshape(B, S, D)

    k_meta = k_meta[META_TILE - N_META:]
    vt_meta = jnp.tile(vt_meta[:, META_TILE - N_META:], (1, SWA_WINDOW // N_META))
    for b in range(n_swa):
        layer = n_ret + b
        final_gain = final_norm.astype(F32).reshape(1, D) if layer == depth - 1 else None
        h = _swa_layer(h, mix_gain, wq16, wo16, swa_sinks[b].astype(F32), slopes, k, vt, k_meta, vt_meta, bias,
                       ffn_gain, ffn_in16, ffn_out16, layer, b, S, tile=TOKEN_TILE, final_gain=final_gain)
    return h.reshape(B, S, D)
```

```python
import functools

import jax
import jax.numpy as jnp
from jax import lax
from jax.experimental import pallas as pl
from jax.experimental.pallas import tpu as pltpu

D_MODEL = 1024
N_META = 16
RET_HEADS = 4
RET_QK_DIM = 256
RET_V_DIM = 512
RET_CHUNK = 128
RET_PROJ = 2 * RET_HEADS * RET_QK_DIM + 2 * RET_HEADS * RET_V_DIM
SWA_HEADS = 16
SWA_KV_HEADS = 4
SWA_GROUP = 4
SWA_HEAD_DIM = 64
SWA_WINDOW = 128
SWA_Q_DIM = SWA_HEADS * SWA_HEAD_DIM
SWA_KV_DIM = SWA_KV_HEADS * SWA_HEAD_DIM
FFN_HIDDEN = 2816
FFN_COL_CHUNK = 256
FFN_OUT_GROUP = 4
RET_PROJ_COL_CHUNK = 256
RMS_EPS = 1e-6
GN_EPS = 1e-6

TOKEN_TILE = 512
FFN_TILE = 1024
META_TILE = RET_CHUNK
RET_MAIN_CHUNK = 256
VMEM_LIMIT_BYTES = 56 * 1024 * 1024

BF16 = jnp.bfloat16
F32 = jnp.float32
NEG_INF = float("-inf")


def _resident(block_shape):
    zeros = (0,) * len(block_shape)
    return pl.BlockSpec(block_shape, lambda *_: zeros, pipeline_mode=pl.Buffered(1))


def _resident_layer(layer, tail_shape):
    index = (layer,) + (0,) * len(tail_shape)
    return pl.BlockSpec((None,) + tuple(tail_shape), lambda *_: index, pipeline_mode=pl.Buffered(1))


def _rms_norm(x, gain):
    ms = jnp.mean(x * x, axis=-1, keepdims=True)
    return x * lax.rsqrt(ms + RMS_EPS) * gain


def _dot(a, b):
    return jnp.dot(a, b, preferred_element_type=F32)


def _dot_nt(a, b):
    return lax.dot_general(a, b, (((1,), (1,)), ((), ())), preferred_element_type=F32)


def _dot_tn(a, b):
    return lax.dot_general(a, b, (((0,), (0,)), ((), ())), preferred_element_type=F32)


def _params(n_grid_axes, flags=None):
    return pltpu.CompilerParams(dimension_semantics=("arbitrary",) * n_grid_axes,
                                vmem_limit_bytes=VMEM_LIMIT_BYTES, flags=flags)


def _ret_tables(C):
    log_gamma = jnp.log1p(-jnp.exp2(-5.0 - jnp.arange(RET_HEADS, dtype=F32)))
    i = jnp.arange(C, dtype=F32)
    diff = i[:, None] - i[None, :]
    dmask = jnp.where(diff >= 0, jnp.exp(log_gamma[:, None, None] * jnp.maximum(diff, 0.0)), 0.0)
    zeta = jnp.exp(log_gamma[:, None] * (C - 1.0 - i)[None, :])
    xi = jnp.exp(log_gamma[:, None] * (i + 1.0)[None, :])
    chunk_decay = jnp.exp(log_gamma * C)
    xi = jnp.broadcast_to(xi[:, :, None], (RET_HEADS, C, RET_QK_DIM))
    zeta = jnp.broadcast_to(zeta[:, :, None], (RET_HEADS, C, RET_QK_DIM))
    cd = jnp.broadcast_to(chunk_decay[:, None, None], (RET_HEADS, 8, RET_V_DIM))
    return dmask, xi, zeta, cd


def _ret_kernel(h_ref, gain_ref, win_ref, wout_ref, dmask_ref, xi_ref, zeta_ref, cd_ref, state0_ref,
                *rest, chunk, n_chunks, emit_state):
    if emit_state:
        o_ref, state_out_ref, state_ref, proj_ref, og_ref = rest
    else:
        o_ref, state_ref, proj_ref, og_ref = rest
    H, dk, dv, C = RET_HEADS, RET_QK_DIM, RET_V_DIM, chunk
    t = pl.program_id(1)

    @pl.when(t == 0)
    def _():
        state_ref[...] = state0_ref[...]

    xn = _rms_norm(h_ref[0], gain_ref[...]).astype(BF16)
    scale = dk ** -0.5
    NC = RET_PROJ_COL_CHUNK

    def head_cols(hh):
        v0, g0 = 2 * H * dk, 2 * H * dk + H * dv
        return (slice(hh * dk, (hh + 1) * dk), slice(H * dk + hh * dk, H * dk + (hh + 1) * dk),
                slice(v0 + hh * dv, v0 + (hh + 1) * dv), slice(g0 + hh * dv, g0 + (hh + 1) * dv))

    def in_proj(hh):
        for cols in head_cols(hh):
            for lo in range(cols.start, cols.stop, NC):
                proj_ref[:, lo:lo + NC] = _dot(xn, win_ref[:, lo:lo + NC])

    def core(hh):
        qc, kc, vc, gc = head_cols(hh)
        for c in range(n_chunks):
            rows = slice(c * C, (c + 1) * C)
            q = proj_ref[rows, qc] * scale
            k = proj_ref[rows, kc]
            v = proj_ref[rows, vc].astype(BF16)
            gate = proj_ref[rows, gc]
            scores = _dot_nt(q.astype(BF16), k.astype(BF16)) * dmask_ref[hh]
            intra = _dot(scores.astype(BF16), v)
            state = state_ref[hh]
            inter = _dot((q * xi_ref[hh]).astype(BF16), state.astype(BF16))
            kz = (k * zeta_ref[hh]).astype(BF16)
            state_ref[hh] = state * cd_ref[hh][0:1, :] + _dot_tn(kz, v)
            o = intra + inter
            mu = jnp.mean(o, axis=-1, keepdims=True)
            oc = o - mu
            var = jnp.mean(oc * oc, axis=-1, keepdims=True)
            on = oc * lax.rsqrt(var + GN_EPS)
            og_ref[rows, hh * dv:(hh + 1) * dv] = (gate * jax.nn.sigmoid(gate) * on).astype(BF16)

    def out_proj(first_head, n_heads, first):
        krows = slice(first_head * dv, (first_head + n_heads) * dv)
        for lo in range(0, D_MODEL, NC):
            part = _dot(og_ref[:, krows], wout_ref[krows, lo:lo + NC])
            base = h_ref[0, :, lo:lo + NC] if first else o_ref[0, :, lo:lo + NC]
            o_ref[0, :, lo:lo + NC] = base + part

    in_proj(0)
    for hh in range(H):
        if hh + 1 < H:
            in_proj(hh + 1)
        core(hh)
        if hh % 2 == 1:
            out_proj(hh - 1, 2, first=hh == 1)

    if emit_state:
        @pl.when(t == pl.num_programs(1) - 1)
        def _():
            state_out_ref[0] = state_ref[...]


def _ret_mixer(h, gain, w_in, w_out, layer, tables, state0, *, tile, emit_state):
    B, L, D = h.shape
    H, dk, dv = RET_HEADS, RET_QK_DIM, RET_V_DIM
    dmask, xi, zeta, cd = tables
    C = dmask.shape[-1]
    out_shape = [jax.ShapeDtypeStruct((B, L, D), F32)]
    out_specs = [pl.BlockSpec((1, tile, D), lambda b, t: (b, t, 0))]
    if emit_state:
        out_shape.append(jax.ShapeDtypeStruct((B, H, dk, dv), F32))
        out_specs.append(pl.BlockSpec((1, H, dk, dv), lambda b, t: (b, 0, 0, 0)))
    res = pl.pallas_call(
        functools.partial(_ret_kernel, chunk=C, n_chunks=tile // C, emit_state=emit_state),
        grid=(B, L // tile),
        in_specs=[
            pl.BlockSpec((1, tile, D), lambda b, t: (b, t, 0)),
            _resident_layer(layer, (1, D)),
            _resident_layer(layer, (D, RET_PROJ)),
            _resident_layer(layer, (H * dv, D)),
            _resident((H, C, C)),
            _resident((H, C, dk)),
            _resident((H, C, dk)),
            _resident((H, 8, dv)),
            _resident((H, dk, dv)),
        ],
        out_specs=out_specs,
        out_shape=out_shape,
        scratch_shapes=[
            pltpu.VMEM((H, dk, dv), F32),
            pltpu.VMEM((tile, RET_PROJ), F32),
            pltpu.VMEM((tile, H * dv), BF16),
        ],
        compiler_params=_params(2),
        name="ret_mixer_meta" if emit_state else "ret_mixer",
    )(h, gain, w_in, w_out, dmask, xi, zeta, cd, state0)
    return res if emit_state else res[0]


def _ffn_body(x, gain_ref, win_ref, wout_ref, act_ref):
    F, CH = FFN_HIDDEN, FFN_COL_CHUNK
    xn = _rms_norm(x, gain_ref[...]).astype(BF16)
    for j in range(F // CH):
        gate = _dot(xn, win_ref[:, j * CH:(j + 1) * CH])
        up = _dot(xn, win_ref[:, F + j * CH:F + (j + 1) * CH])
        act_ref[:, j * CH:(j + 1) * CH] = (gate * jax.nn.sigmoid(gate) * up).astype(BF16)
    return x + _dot(act_ref[...], wout_ref[...])


def _ffn_kernel(h_ref, gain_ref, win_ref, wout_ref, *rest, with_kv):
    if with_kv:
        kv_gain_ref, wk_ref, wvt_ref, o_ref, k_ref, vt_ref, act_ref = rest
    else:
        o_ref, act_ref = rest
    y = _ffn_body(h_ref[...], gain_ref, win_ref, wout_ref, act_ref)
    o_ref[...] = y
    if with_kv:
        xn = _rms_norm(y, kv_gain_ref[...]).astype(BF16)
        k_ref[...] = _dot(xn, wk_ref[...]).astype(BF16)
        vt_ref[...] = _dot_nt(wvt_ref[...], xn).astype(BF16)


def _ffn(h, gain, w_in, w_out, layer, *, tile, kv_gain=None, wk=None, wvt=None):
    T, D = h.shape
    F, KD = FFN_HIDDEN, SWA_KV_DIM
    with_kv = wk is not None
    in_specs = [
        pl.BlockSpec((tile, D), lambda t: (t, 0)),
        _resident_layer(layer, (1, D)),
        _resident_layer(layer, (D, 2 * F)),
        _resident_layer(layer, (F, D)),
    ]
    args = [h, gain, w_in, w_out]
    out_shape = [jax.ShapeDtypeStruct((T, D), F32)]
    out_specs = [pl.BlockSpec((tile, D), lambda t: (t, 0))]
    if with_kv:
        in_specs += [_resident((1, D)), _resident((D, KD)), _resident((KD, D))]
        args += [kv_gain, wk, wvt]
        out_shape += [jax.ShapeDtypeStruct((T, KD), BF16), jax.ShapeDtypeStruct((KD, T), BF16)]
        out_specs += [pl.BlockSpec((tile, KD), lambda t: (t, 0)), pl.BlockSpec((KD, tile), lambda t: (0, t))]
    res = pl.pallas_call(
        functools.partial(_ffn_kernel, with_kv=with_kv),
        grid=(T // tile,),
        in_specs=in_specs,
        out_specs=out_specs,
        out_shape=out_shape,
        scratch_shapes=[pltpu.VMEM((tile, F), BF16)],
        compiler_params=_params(1),
        name="ffn_kv" if with_kv else "ffn",
    )(*args)
    return res if with_kv else res[0]


def _swa_bias_tables():
    W = SWA_WINDOW
    slopes = jnp.exp2(-8.0 * (jnp.arange(SWA_HEADS, dtype=F32) + 1.0) / SWA_HEADS)
    c = jnp.arange(W)[:, None]
    r = jnp.arange(W)[None, :]
    s = slopes[:, None, None]
    b_prev = jnp.where((c > r)[None], -s * (W + r - c).astype(F32)[None], NEG_INF)
    b_cur = jnp.where((c <= r)[None], -s * (r - c).astype(F32)[None], NEG_INF)
    m = jnp.arange(N_META)[:, None]
    b_meta = -s * (N_META + r - m).astype(F32)[None]
    return jnp.concatenate([b_prev, b_meta, b_cur], axis=1), slopes


def _swa_steps(h_ref, t, sinks_ref, slopes_ref, gain_ref, wq_ref, wo_ref, kp_ref, kc_ref, vtp_ref, vtc_ref,
               kmeta_ref, vtmeta_ref, bias_ref, q_ref, km_ref, vmt_ref, st_ref, pt_ref, scale_ref, attt_ref,
               mid_ref, *, n_blocks):
    W, G, KVH, KD, HD = SWA_WINDOW, SWA_GROUP, SWA_KV_HEADS, SWA_KV_DIM, SWA_HEAD_DIM
    BLK = KVH * W
    MET = W
    STRIDE = BLK + MET
    WIN = 2 * BLK + MET
    half_blocks = n_blocks // 2

    def q_proj(first_block):
        rows = slice(first_block * W, (first_block + half_blocks) * W)
        xn = _rms_norm(h_ref[rows, :], gain_ref[...]).astype(BF16)
        for g in range(G):
            qg = (_dot(xn, wq_ref[:, g * KD:(g + 1) * KD]) * (HD ** -0.5)).astype(BF16)
            for bb in range(half_blocks):
                q_ref[first_block + bb, g * W:(g + 1) * W, :] = qg[bb * W:(bb + 1) * W, :]

    def o_proj(first_block):
        rows = slice(first_block * W, (first_block + half_blocks) * W)
        mid_ref[rows, :] = h_ref[rows, :] + _dot_tn(attt_ref[:, rows], wo_ref[...])

    q_proj(0)
    yield False

    col_kv = lax.broadcasted_iota(jnp.int32, (W, KD), 1) // HD
    col_kv_meta = lax.broadcasted_iota(jnp.int32, (N_META, KD), 1) // HD
    row_kv = lax.broadcasted_iota(jnp.int32, (KD, W), 0) // HD
    lane = lax.broadcasted_iota(jnp.int32, (KD, W), 1)
    zero = jnp.zeros((), BF16)
    kmeta_bd = jnp.concatenate(
        [jnp.where(col_kv_meta == kv, kmeta_ref[...], zero) for kv in range(KVH)]
        + [jnp.zeros((MET - KVH * N_META, KD), BF16)], axis=0)
    vtmeta_bd = jnp.where((lane // N_META == row_kv) & (lane < KVH * N_META), vtmeta_ref[...], zero)
    for j in range(n_blocks + 1):
        kblk = kp_ref[...] if j == 0 else kc_ref[(j - 1) * W:j * W, :]
        vblk = vtp_ref[...] if j == 0 else vtc_ref[:, (j - 1) * W:j * W]
        for kv in range(KVH):
            span = slice(j * STRIDE + kv * W, j * STRIDE + (kv + 1) * W)
            km_ref[span, :] = jnp.where(col_kv == kv, kblk, zero)
            vmt_ref[:, span] = jnp.where(row_kv == kv, vblk, zero)
        if j < n_blocks:
            span = slice(j * STRIDE + BLK, (j + 1) * STRIDE)
            km_ref[span, :] = kmeta_bd
            vmt_ref[:, span] = vtmeta_bd
    yield False

    first_pen = jnp.where(t == 0, NEG_INF, 0.0).astype(F32)
    for b in range(n_blocks):
        if b == half_blocks:
            q_proj(half_blocks)
            yield False
        slot = b % 2
        window = slice(b * STRIDE, b * STRIDE + WIN)
        st_ref[slot] = _dot_nt(km_ref[window, :], q_ref[b])
        pt_ref[slot, BLK + KVH * N_META:STRIDE, :] = jnp.zeros((MET - KVH * N_META, G * W), BF16)
        yield False
        block_index = (t * n_blocks + b).astype(F32)
        for kv in range(KVH):
            prev_rows = slice(kv * W, (kv + 1) * W)
            meta_rows = slice(BLK + kv * N_META, BLK + (kv + 1) * N_META)
            cur_rows = slice(STRIDE + kv * W, STRIDE + (kv + 1) * W)
            for g in range(G):
                head = kv * G + g
                cols = slice(g * W, (g + 1) * W)
                sink = sinks_ref[head]
                meta_shift = -(slopes_ref[head] * float(W)) * block_index
                s_prev = st_ref[slot, prev_rows, cols] + bias_ref[head, 0:W, :]
                if b == 0:
                    s_prev = s_prev + first_pen
                s_meta = st_ref[slot, meta_rows, cols] + (bias_ref[head, W:W + N_META, :] + meta_shift)
                s_cur = st_ref[slot, cur_rows, cols] + bias_ref[head, W + N_META:, :]
                m = jnp.maximum(jnp.maximum(jnp.max(s_prev, axis=0, keepdims=True),
                                            jnp.max(s_cur, axis=0, keepdims=True)),
                                jnp.max(s_meta, axis=0, keepdims=True))
                m = jnp.maximum(m, sink)
                e_prev = jnp.exp(s_prev - m)
                e_meta = jnp.exp(s_meta - m)
                e_cur = jnp.exp(s_cur - m)
                denom = (jnp.sum(e_prev, axis=0, keepdims=True) + jnp.sum(e_cur, axis=0, keepdims=True)
                         + jnp.sum(e_meta, axis=0, keepdims=True) + jnp.exp(sink - m))
                pt_ref[slot, prev_rows, cols] = e_prev.astype(BF16)
                pt_ref[slot, meta_rows, cols] = e_meta.astype(BF16)
                pt_ref[slot, cur_rows, cols] = e_cur.astype(BF16)
                scale_ref[slot, kv * HD:(kv + 1) * HD, cols] = jnp.broadcast_to(1.0 / denom, (HD, W))
            yield True
        out_t = _dot(vmt_ref[:, window], pt_ref[slot]) * scale_ref[slot]
        for g in range(G):
            attt_ref[g * KD:(g + 1) * KD, b * W:(b + 1) * W] = out_t[:, g * W:(g + 1) * W].astype(BF16)
        yield False
        if b + 1 == half_blocks:
            o_proj(0)
            yield False
    o_proj(half_blocks)
    yield False


def _ffn_steps(x_ref, gain_ref, win_ref, wout_ref, act_ref, o_ref):
    F, CH = FFN_HIDDEN, FFN_COL_CHUNK
    n_chunks = F // CH
    xn = _rms_norm(x_ref[...], gain_ref[...]).astype(BF16)
    first = True
    for j in range(n_chunks):
        gate = _dot(xn, win_ref[:, j * CH:(j + 1) * CH])
        up = _dot(xn, win_ref[:, F + j * CH:F + (j + 1) * CH])
        act_ref[:, j * CH:(j + 1) * CH] = (gate * jax.nn.sigmoid(gate) * up).astype(BF16)
        yield
        if (j + 1) % FFN_OUT_GROUP == 0 or j == n_chunks - 1:
            lo = (j // FFN_OUT_GROUP) * FFN_OUT_GROUP * CH
            for n in range(D_MODEL // CH):
                cols = slice(n * CH, (n + 1) * CH)
                part = _dot(act_ref[:, lo:(j + 1) * CH], wout_ref[lo:(j + 1) * CH, cols])
                o_ref[:, cols] = (x_ref[:, cols] if first else o_ref[:, cols]) + part
            first = False
            yield


def _swa_layer_kernel(sinks_ref, slopes_ref, h_ref, gain_ref, wq_ref, wo_ref, kp_ref, kc_ref, vtp_ref, vtc_ref,
                      kmeta_ref, vtmeta_ref, bias_ref, fgain_ref, fwin_ref, fwout_ref, *rest,
                      n_blocks, tiles_per_seq, final_norm):
    if final_norm:
        final_gain_ref, o_ref, *scratch = rest
    else:
        o_ref, *scratch = rest
    mid_ref, act_ref = scratch[-2:]
    i = pl.program_id(0)

    @pl.when(i == 0)
    def _():
        mid_ref[1] = jnp.zeros((mid_ref.shape[1], mid_ref.shape[2]), F32)

    t = lax.rem(jnp.minimum(i, pl.num_programs(0) - 2), tiles_per_seq)
    slot = lax.rem(i, 2)
    ffn = _ffn_steps(mid_ref.at[1 - slot], fgain_ref, fwin_ref, fwout_ref, act_ref, o_ref)
    mixer = _swa_steps(h_ref, t, sinks_ref, slopes_ref, gain_ref, wq_ref, wo_ref, kp_ref, kc_ref, vtp_ref, vtc_ref,
                       kmeta_ref, vtmeta_ref, bias_ref, *scratch[:-2], mid_ref.at[slot], n_blocks=n_blocks)
    next(ffn, None)
    for was_softmax in mixer:
        if was_softmax:
            next(ffn, None)
    for _ in ffn:
        pass
    if final_norm:
        o_ref[...] = _rms_norm(o_ref[...], final_gain_ref[...])


def _swa_layer(h, mix_gain, wq, wo, sinks, slopes, k, vt, k_meta, vt_meta, bias, ffn_gain, f_in, f_out, layer,
               swa_index, seq_len, *, tile, final_gain=None):
    T, D = h.shape
    W, G, KVH, KD, F = SWA_WINDOW, SWA_GROUP, SWA_KV_HEADS, SWA_KV_DIM, FFN_HIDDEN
    n_blocks = tile // W
    n_tiles = T // tile
    stride = (KVH + 1) * W
    final_norm = final_gain is not None
    smem = pl.BlockSpec(memory_space=pltpu.SMEM)
    cur = lambda i: jnp.minimum(i, n_tiles - 1)
    prev_block = lambda i: jnp.maximum(cur(i) * n_blocks - 1, 0)
    in_specs = [
        smem, smem,
        pl.BlockSpec((tile, D), lambda i: (cur(i), 0)),
        _resident_layer(layer, (1, D)),
        _resident_layer(swa_index, (D, SWA_Q_DIM)),
        _resident_layer(swa_index, (SWA_Q_DIM, D)),
        pl.BlockSpec((W, KD), lambda i: (prev_block(i), 0)),
        pl.BlockSpec((tile, KD), lambda i: (cur(i), 0)),
        pl.BlockSpec((KD, W), lambda i: (0, prev_block(i))),
        pl.BlockSpec((KD, tile), lambda i: (0, cur(i))),
        _resident((N_META, KD)),
        _resident((KD, W)),
        _resident((SWA_HEADS, 2 * W + N_META, W)),
        _resident_layer(layer, (1, D)),
        _resident_layer(layer, (D, 2 * F)),
        _resident_layer(layer, (F, D)),
    ]
    args = [sinks, slopes, h, mix_gain, wq, wo, k, k, vt, vt, k_meta, vt_meta, bias, ffn_gain, f_in, f_out]
    if final_norm:
        in_specs.append(_resident((1, D)))
        args.append(final_gain)
    return pl.pallas_call(
        functools.partial(_swa_layer_kernel, n_blocks=n_blocks, tiles_per_seq=seq_len // tile,
                          final_norm=final_norm),
        grid=(n_tiles + 1,),
        in_specs=in_specs,
        out_specs=pl.BlockSpec((tile, D), lambda i: (jnp.maximum(i - 1, 0), 0)),
        out_shape=jax.ShapeDtypeStruct((T, D), F32),
        scratch_shapes=[
            pltpu.VMEM((n_blocks, G * W, KD), BF16),
            pltpu.VMEM((n_blocks * stride + KVH * W, KD), BF16),
            pltpu.VMEM((KD, n_blocks * stride + KVH * W), BF16),
            pltpu.VMEM((2, stride + KVH * W, G * W), F32),
            pltpu.VMEM((2, stride + KVH * W, G * W), BF16),
            pltpu.VMEM((2, KD, G * W), F32),
            pltpu.VMEM((SWA_Q_DIM, tile), BF16),
            pltpu.VMEM((2, tile, D), F32),
            pltpu.VMEM((tile, F), BF16),
        ],
        compiler_params=_params(1),
        name="swa_layer_final" if final_norm else "swa_layer",
    )(*args)


def _group_major(w, axis):
    shape = w.shape
    split = shape[:axis] + (SWA_KV_HEADS, SWA_GROUP, SWA_HEAD_DIM) + shape[axis + 1:]
    return jnp.swapaxes(w.reshape(split), axis, axis + 1).reshape(shape)


def kernel(x, meta_tokens, mix_norm, ffn_norm, ret_w_in, ret_w_out, kv_norm, kv_w, swa_w_q, swa_w_o,
           swa_sinks, ffn_w_in, ffn_w_out, final_norm):
    B, S, D = x.shape
    n_ret = ret_w_in.shape[0]
    n_swa = swa_w_q.shape[0]
    depth = n_ret + n_swa
    assert D == D_MODEL and S % TOKEN_TILE == 0

    meta_tables = _ret_tables(META_TILE)
    tables = _ret_tables(RET_MAIN_CHUNK)
    bias, slopes = _swa_bias_tables()
    mix_gain = mix_norm.astype(F32).reshape(depth, 1, D)
    ffn_gain = ffn_norm.astype(F32).reshape(depth, 1, D)
    ret_in16, ret_out16 = ret_w_in.astype(BF16), ret_w_out.astype(BF16)
    ffn_in16, ffn_out16 = ffn_w_in.astype(BF16), ffn_w_out.astype(BF16)
    wq16 = _group_major(swa_w_q, 2).astype(BF16)
    wo16 = _group_major(swa_w_o, 1).astype(BF16)
    wk16 = kv_w[:, :SWA_KV_DIM].astype(BF16)
    wvt16 = kv_w[:, SWA_KV_DIM:].T.astype(BF16)
    kv_gain = kv_norm.astype(F32).reshape(1, D)

    hm = jnp.concatenate([jnp.zeros((META_TILE - N_META, D), F32), meta_tokens.astype(F32)], axis=0)
    h = x.astype(F32)
    state0 = jnp.zeros((RET_HEADS, RET_QK_DIM, RET_V_DIM), F32)
    k = vt = k_meta = vt_meta = None

    for layer in range(n_ret):
        hm, state_meta = _ret_mixer(hm[None], mix_gain, ret_in16, ret_out16, layer, meta_tables, state0,
                                    tile=META_TILE, emit_state=True)
        h = _ret_mixer(h, mix_gain, ret_in16, ret_out16, layer, tables, state_meta[0],
                       tile=TOKEN_TILE, emit_state=False)
        if layer == n_ret - 1:
            hm, k_meta, vt_meta = _ffn(hm[0], ffn_gain, ffn_in16, ffn_out16, layer, tile=META_TILE,
                                       kv_gain=kv_gain, wk=wk16, wvt=wvt16)
            h, k, vt = _ffn(h.reshape(B * S, D), ffn_gain, ffn_in16, ffn_out16, layer, tile=FFN_TILE,
                            kv_gain=kv_gain, wk=wk16, wvt=wvt16)
        else:
            hm = _ffn(hm[0], ffn_gain, ffn_in16, ffn_out16, layer, tile=META_TILE)
            h = _ffn(h.reshape(B * S, D), ffn_gain, ffn_in16, ffn_out16, layer, tile=FFN_TILE).reshape(B, S, D)

    k_meta = k_meta[META_TILE - N_META:]
    vt_meta = jnp.tile(vt_meta[:, META_TILE - N_META:], (1, SWA_WINDOW // N_META))
    for b in range(n_swa):
        layer = n_ret + b
        final_gain = final_norm.astype(F32).reshape(1, D) if layer == depth - 1 else None
        h = _swa_layer(h, mix_gain, wq16, wo16, swa_sinks[b].astype(F32), slopes, k, vt, k_meta, vt_meta, bias,
                       ffn_gain, ffn_in16, ffn_out16, layer, b, S, tile=TOKEN_TILE, final_gain=final_gain)
    return h.reshape(B, S, D)
```

```python
import functools

import jax
import jax.numpy as jnp
from jax import lax
from jax.experimental import pallas as pl
from jax.experimental.pallas import tpu as pltpu

D_MODEL = 1024
N_META = 16
RET_HEADS = 4
RET_QK_DIM = 256
RET_V_DIM = 512
RET_CHUNK = 128
RET_PROJ = 2 * RET_HEADS * RET_QK_DIM + 2 * RET_HEADS * RET_V_DIM
SWA_HEADS = 16
SWA_KV_HEADS = 4
SWA_GROUP = 4
SWA_HEAD_DIM = 64
SWA_WINDOW = 128
SWA_Q_DIM = SWA_HEADS * SWA_HEAD_DIM
SWA_KV_DIM = SWA_KV_HEADS * SWA_HEAD_DIM
SWA_SUM_ROWS = 16
FFN_HIDDEN = 2816
FFN_COL_CHUNK = 256
FFN_OUT_GROUP = 4
RET_PROJ_COL_CHUNK = 256
RMS_EPS = 1e-6
GN_EPS = 1e-6

TOKEN_TILE = 512
FFN_TILE = 1024
META_TILE = RET_CHUNK
RET_MAIN_CHUNK = 256
VMEM_LIMIT_BYTES = 56 * 1024 * 1024

BF16 = jnp.bfloat16
F32 = jnp.float32
NEG_INF = float("-inf")
LOG2E = 1.4426950408889634


def _resident(block_shape):
    zeros = (0,) * len(block_shape)
    return pl.BlockSpec(block_shape, lambda *_: zeros, pipeline_mode=pl.Buffered(1))


def _resident_layer(layer, tail_shape):
    index = (layer,) + (0,) * len(tail_shape)
    return pl.BlockSpec((None,) + tuple(tail_shape), lambda *_: index, pipeline_mode=pl.Buffered(1))


def _rms_norm(x, gain):
    ms = jnp.mean(x * x, axis=-1, keepdims=True)
    return x * lax.rsqrt(ms + RMS_EPS) * gain


def _silu(x):
    half = 0.5 * x
    return half + half * jnp.tanh(half)


def _dot(a, b):
    return jnp.dot(a, b, preferred_element_type=F32)


def _dot_nt(a, b):
    return lax.dot_general(a, b, (((1,), (1,)), ((), ())), preferred_element_type=F32)


def _dot_tn(a, b):
    return lax.dot_general(a, b, (((0,), (0,)), ((), ())), preferred_element_type=F32)


def _params(n_grid_axes, flags=None):
    return pltpu.CompilerParams(dimension_semantics=("arbitrary",) * n_grid_axes,
                                vmem_limit_bytes=VMEM_LIMIT_BYTES, flags=flags)


def _ret_tables(C):
    log_gamma = jnp.log1p(-jnp.exp2(-5.0 - jnp.arange(RET_HEADS, dtype=F32)))
    i = jnp.arange(C, dtype=F32)
    diff = i[:, None] - i[None, :]
    dmask = jnp.where(diff >= 0, jnp.exp(log_gamma[:, None, None] * jnp.maximum(diff, 0.0)), 0.0)
    zeta = jnp.exp(log_gamma[:, None] * (C - 1.0 - i)[None, :])
    xi = jnp.exp(log_gamma[:, None] * (i + 1.0)[None, :])
    chunk_decay = jnp.exp(log_gamma * C)
    xi = jnp.broadcast_to(xi[:, :, None], (RET_HEADS, C, RET_QK_DIM))
    zeta = jnp.broadcast_to(zeta[:, :, None], (RET_HEADS, C, RET_QK_DIM))
    cd = jnp.broadcast_to(chunk_decay[:, None, None], (RET_HEADS, 8, RET_V_DIM))
    return dmask, xi, zeta, cd


def _ret_kernel(h_ref, gain_ref, win_ref, wout_ref, dmask_ref, xi_ref, zeta_ref, cd_ref, state0_ref,
                *rest, chunk, n_chunks, emit_state):
    if emit_state:
        o_ref, state_out_ref, state_ref, proj_ref, og_ref = rest
    else:
        o_ref, state_ref, proj_ref, og_ref = rest
    H, dk, dv, C = RET_HEADS, RET_QK_DIM, RET_V_DIM, chunk
    t = pl.program_id(1)

    @pl.when(t == 0)
    def _():
        state_ref[...] = state0_ref[...]

    xn = _rms_norm(h_ref[0], gain_ref[...]).astype(BF16)
    scale = dk ** -0.5
    NC = RET_PROJ_COL_CHUNK

    def head_cols(hh):
        v0, g0 = 2 * H * dk, 2 * H * dk + H * dv
        return (slice(hh * dk, (hh + 1) * dk), slice(H * dk + hh * dk, H * dk + (hh + 1) * dk),
                slice(v0 + hh * dv, v0 + (hh + 1) * dv), slice(g0 + hh * dv, g0 + (hh + 1) * dv))

    def in_proj(hh):
        for cols in head_cols(hh):
            for lo in range(cols.start, cols.stop, NC):
                proj_ref[:, lo:lo + NC] = _dot(xn, win_ref[:, lo:lo + NC])

    def core(hh):
        qc, kc, vc, gc = head_cols(hh)
        for c in range(n_chunks):
            rows = slice(c * C, (c + 1) * C)
            q = proj_ref[rows, qc] * scale
            k = proj_ref[rows, kc]
            v = proj_ref[rows, vc].astype(BF16)
            gate = proj_ref[rows, gc]
            scores = _dot_nt(q.astype(BF16), k.astype(BF16)) * dmask_ref[hh]
            intra = _dot(scores.astype(BF16), v)
            state = state_ref[hh]
            inter = _dot((q * xi_ref[hh]).astype(BF16), state.astype(BF16))
            kz = (k * zeta_ref[hh]).astype(BF16)
            state_ref[hh] = state * cd_ref[hh][0:1, :] + _dot_tn(kz, v)
            o = intra + inter
            mu = jnp.mean(o, axis=-1, keepdims=True)
            oc = o - mu
            var = jnp.mean(oc * oc, axis=-1, keepdims=True)
            on = oc * lax.rsqrt(var + GN_EPS)
            og_ref[rows, hh * dv:(hh + 1) * dv] = (_silu(gate) * on).astype(BF16)

    def out_proj(first_head, n_heads, first):
        krows = slice(first_head * dv, (first_head + n_heads) * dv)
        for lo in range(0, D_MODEL, NC):
            part = _dot(og_ref[:, krows], wout_ref[krows, lo:lo + NC])
            base = h_ref[0, :, lo:lo + NC] if first else o_ref[0, :, lo:lo + NC]
            o_ref[0, :, lo:lo + NC] = base + part

    in_proj(0)
    for hh in range(H):
        if hh + 1 < H:
            in_proj(hh + 1)
        core(hh)
        if hh % 2 == 1:
            out_proj(hh - 1, 2, first=hh == 1)

    if emit_state:
        @pl.when(t == pl.num_programs(1) - 1)
        def _():
            state_out_ref[0] = state_ref[...]


def _ret_mixer(h, gain, w_in, w_out, layer, tables, state0, *, tile, emit_state):
    B, L, D = h.shape
    H, dk, dv = RET_HEADS, RET_QK_DIM, RET_V_DIM
    dmask, xi, zeta, cd = tables
    C = dmask.shape[-1]
    out_shape = [jax.ShapeDtypeStruct((B, L, D), F32)]
    out_specs = [pl.BlockSpec((1, tile, D), lambda b, t: (b, t, 0))]
    if emit_state:
        out_shape.append(jax.ShapeDtypeStruct((B, H, dk, dv), F32))
        out_specs.append(pl.BlockSpec((1, H, dk, dv), lambda b, t: (b, 0, 0, 0)))
    res = pl.pallas_call(
        functools.partial(_ret_kernel, chunk=C, n_chunks=tile // C, emit_state=emit_state),
        grid=(B, L // tile),
        in_specs=[
            pl.BlockSpec((1, tile, D), lambda b, t: (b, t, 0)),
            _resident_layer(layer, (1, D)),
            _resident_layer(layer, (D, RET_PROJ)),
            _resident_layer(layer, (H * dv, D)),
            _resident((H, C, C)),
            _resident((H, C, dk)),
            _resident((H, C, dk)),
            _resident((H, 8, dv)),
            _resident((H, dk, dv)),
        ],
        out_specs=out_specs,
        out_shape=out_shape,
        scratch_shapes=[
            pltpu.VMEM((H, dk, dv), F32),
            pltpu.VMEM((tile, RET_PROJ), F32),
            pltpu.VMEM((tile, H * dv), BF16),
        ],
        compiler_params=_params(2),
        name="ret_mixer_meta" if emit_state else "ret_mixer",
    )(h, gain, w_in, w_out, dmask, xi, zeta, cd, state0)
    return res if emit_state else res[0]


def _ffn_body(x, gain_ref, win_ref, wout_ref, act_ref):
    F, CH = FFN_HIDDEN, FFN_COL_CHUNK
    xn = _rms_norm(x, gain_ref[...]).astype(BF16)
    for j in range(F // CH):
        gate = _dot(xn, win_ref[:, j * CH:(j + 1) * CH])
        up = _dot(xn, win_ref[:, F + j * CH:F + (j + 1) * CH])
        act_ref[:, j * CH:(j + 1) * CH] = (_silu(gate) * up).astype(BF16)
    return x + _dot(act_ref[...], wout_ref[...])


def _ffn_kernel(h_ref, gain_ref, win_ref, wout_ref, *rest, with_kv):
    if with_kv:
        kv_gain_ref, wk_ref, wvt_ref, o_ref, k_ref, vt_ref, act_ref = rest
    else:
        o_ref, act_ref = rest
    y = _ffn_body(h_ref[...], gain_ref, win_ref, wout_ref, act_ref)
    o_ref[...] = y
    if with_kv:
        xn = _rms_norm(y, kv_gain_ref[...]).astype(BF16)
        k_ref[...] = _dot(xn, wk_ref[...]).astype(BF16)
        vt_ref[...] = _dot_nt(wvt_ref[...], xn).astype(BF16)


def _ffn(h, gain, w_in, w_out, layer, *, tile, kv_gain=None, wk=None, wvt=None):
    T, D = h.shape
    F, KD = FFN_HIDDEN, SWA_KV_DIM
    with_kv = wk is not None
    in_specs = [
        pl.BlockSpec((tile, D), lambda t: (t, 0)),
        _resident_layer(layer, (1, D)),
        _resident_layer(layer, (D, 2 * F)),
        _resident_layer(layer, (F, D)),
    ]
    args = [h, gain, w_in, w_out]
    out_shape = [jax.ShapeDtypeStruct((T, D), F32)]
    out_specs = [pl.BlockSpec((tile, D), lambda t: (t, 0))]
    if with_kv:
        in_specs += [_resident((1, D)), _resident((D, KD)), _resident((KD, D))]
        args += [kv_gain, wk, wvt]
        out_shape += [jax.ShapeDtypeStruct((T, KD), BF16), jax.ShapeDtypeStruct((KD, T), BF16)]
        out_specs += [pl.BlockSpec((tile, KD), lambda t: (t, 0)), pl.BlockSpec((KD, tile), lambda t: (0, t))]
    res = pl.pallas_call(
        functools.partial(_ffn_kernel, with_kv=with_kv),
        grid=(T // tile,),
        in_specs=in_specs,
        out_specs=out_specs,
        out_shape=out_shape,
        scratch_shapes=[pltpu.VMEM((tile, F), BF16)],
        compiler_params=_params(1),
        name="ffn_kv" if with_kv else "ffn",
    )(*args)
    return res if with_kv else res[0]


def _swa_bias_tables():
    W = SWA_WINDOW
    slopes = jnp.exp2(-8.0 * (jnp.arange(SWA_HEADS, dtype=F32) + 1.0) / SWA_HEADS)
    c = jnp.arange(W)[:, None]
    r = jnp.arange(W)[None, :]
    s = slopes[:, None, None]
    b_prev = jnp.where((c > r)[None], -s * (W + r - c).astype(F32)[None], NEG_INF)
    b_cur = jnp.where((c <= r)[None], -s * (r - c).astype(F32)[None], NEG_INF)
    m = jnp.arange(N_META)[:, None]
    b_meta = -s * (N_META + r - m).astype(F32)[None]
    return jnp.concatenate([b_prev, b_meta, b_cur], axis=1), slopes


def _swa_steps(h_ref, t, sinks_ref, slopes_ref, gain_ref, wq_ref, wo_ref, kp_ref, kc_ref, vtp_ref, vtc_ref,
               kmeta_ref, vtmeta_ref, bias_ref, q_ref, km_ref, vmt_ref, st_ref, pt_ref, attt_ref,
               mid_ref, *, n_blocks):
    W, G, KVH, KD, HD = SWA_WINDOW, SWA_GROUP, SWA_KV_HEADS, SWA_KV_DIM, SWA_HEAD_DIM
    SUM_ROWS = SWA_SUM_ROWS
    BLK = KVH * W
    MET = W
    STRIDE = BLK + MET
    WIN = 2 * BLK + MET
    half_blocks = n_blocks // 2

    def q_proj(first_block):
        rows = slice(first_block * W, (first_block + half_blocks) * W)
        xn = _rms_norm(h_ref[rows, :], gain_ref[...]).astype(BF16)
        for g in range(G):
            qg = (_dot(xn, wq_ref[:, g * KD:(g + 1) * KD]) * (HD ** -0.5 * LOG2E)).astype(BF16)
            for bb in range(half_blocks):
                q_ref[first_block + bb, g * W:(g + 1) * W, :] = qg[bb * W:(bb + 1) * W, :]

    def o_proj(first_block):
        rows = slice(first_block * W, (first_block + half_blocks) * W)
        mid_ref[rows, :] = h_ref[rows, :] + _dot_tn(attt_ref[:, rows], wo_ref[...])

    q_proj(0)
    yield False

    col_kv = lax.broadcasted_iota(jnp.int32, (W, KD), 1) // HD
    col_kv_meta = lax.broadcasted_iota(jnp.int32, (N_META, KD), 1) // HD
    row_kv = lax.broadcasted_iota(jnp.int32, (KD, W), 0) // HD
    lane = lax.broadcasted_iota(jnp.int32, (KD, W), 1)
    zero = jnp.zeros((), BF16)
    kmeta_bd = jnp.concatenate(
        [jnp.where(col_kv_meta == kv, kmeta_ref[...], zero) for kv in range(KVH)]
        + [jnp.zeros((MET - KVH * N_META, KD), BF16)], axis=0)
    vtmeta_bd = jnp.where((lane // N_META == row_kv) & (lane < KVH * N_META), vtmeta_ref[...], zero)
    ind_row = lax.broadcasted_iota(jnp.int32, (SUM_ROWS, W), 0)
    ind_lane = lax.broadcasted_iota(jnp.int32, (SUM_ROWS, W), 1)
    ind_meta = ((ind_lane // N_META == ind_row) & (ind_lane < KVH * N_META)).astype(BF16)
    for j in range(n_blocks + 1):
        kblk = kp_ref[...] if j == 0 else kc_ref[(j - 1) * W:j * W, :]
        vblk = vtp_ref[...] if j == 0 else vtc_ref[:, (j - 1) * W:j * W]
        for kv in range(KVH):
            span = slice(j * STRIDE + kv * W, j * STRIDE + (kv + 1) * W)
            km_ref[span, :] = jnp.where(col_kv == kv, kblk, zero)
            vmt_ref[:KD, span] = jnp.where(row_kv == kv, vblk, zero)
            vmt_ref[KD:, span] = (ind_row == kv).astype(BF16)
        if j < n_blocks:
            span = slice(j * STRIDE + BLK, (j + 1) * STRIDE)
            km_ref[span, :] = kmeta_bd
            vmt_ref[:KD, span] = vtmeta_bd
            vmt_ref[KD:, span] = ind_meta
    yield False

    def logits(b):
        window = slice(b * STRIDE, b * STRIDE + WIN)
        st_ref[b % 2] = _dot_nt(km_ref[window, :], q_ref[b])
        pt_ref[b % 2, BLK + KVH * N_META:STRIDE, :] = jnp.zeros((MET - KVH * N_META, G * W), BF16)

    first_pen = jnp.where(t == 0, NEG_INF, 0.0).astype(F32)
    for b in range(n_blocks):
        if b == half_blocks:
            q_proj(half_blocks)
            yield False
        logits(b)
        yield False
        slot = b % 2
        window = slice(b * STRIDE, b * STRIDE + WIN)
        block_index = (t * n_blocks + b).astype(F32)
        sink_terms = []
        for kv in range(KVH):
            prev_rows = slice(kv * W, (kv + 1) * W)
            meta_rows = slice(BLK + kv * N_META, BLK + (kv + 1) * N_META)
            cur_rows = slice(STRIDE + kv * W, STRIDE + (kv + 1) * W)
            sink_row = []
            for g in range(G):
                head = kv * G + g
                cols = slice(g * W, (g + 1) * W)
                sink = sinks_ref[head]
                meta_shift = -(slopes_ref[head] * float(W)) * block_index
                s_prev = st_ref[slot, prev_rows, cols] + bias_ref[head, 0:W, :]
                if b == 0:
                    s_prev = s_prev + first_pen
                s_meta = st_ref[slot, meta_rows, cols] + (bias_ref[head, W:W + N_META, :] + meta_shift)
                s_cur = st_ref[slot, cur_rows, cols] + bias_ref[head, W + N_META:, :]
                m = jnp.maximum(jnp.maximum(jnp.max(s_prev, axis=0, keepdims=True),
                                            jnp.max(s_cur, axis=0, keepdims=True)),
                                jnp.max(s_meta, axis=0, keepdims=True))
                m = jnp.maximum(m, sink)
                pt_ref[slot, prev_rows, cols] = jnp.exp2(s_prev - m).astype(BF16)
                pt_ref[slot, meta_rows, cols] = jnp.exp2(s_meta - m).astype(BF16)
                pt_ref[slot, cur_rows, cols] = jnp.exp2(s_cur - m).astype(BF16)
                sink_row.append(jnp.exp2(sink - m))
            sink_terms.append(jnp.concatenate(sink_row, axis=1))
            yield True
        out_all = _dot(vmt_ref[:, window], pt_ref[slot])
        inv = 1.0 / (out_all[KD:KD + KVH, :] + jnp.concatenate(sink_terms, axis=0))
        for g in range(G):
            cols = slice(g * W, (g + 1) * W)
            out_g = jnp.concatenate([out_all[kv * HD:(kv + 1) * HD, cols] * inv[kv:kv + 1, cols]
                                     for kv in range(KVH)], axis=0)
            attt_ref[g * KD:(g + 1) * KD, b * W:(b + 1) * W] = out_g.astype(BF16)
        yield False
        if b + 1 == half_blocks:
            o_proj(0)
            yield False
    o_proj(half_blocks)
    yield False


def _ffn_steps(x_ref, gain_ref, win_ref, wout_ref, act_ref, o_ref):
    F, CH = FFN_HIDDEN, FFN_COL_CHUNK
    n_chunks = F // CH
    xn = _rms_norm(x_ref[...], gain_ref[...]).astype(BF16)
    first = True
    for j in range(n_chunks):
        gate = _dot(xn, win_ref[:, j * CH:(j + 1) * CH])
        up = _dot(xn, win_ref[:, F + j * CH:F + (j + 1) * CH])
        act_ref[:, j * CH:(j + 1) * CH] = (_silu(gate) * up).astype(BF16)
        yield
        if (j + 1) % FFN_OUT_GROUP == 0 or j == n_chunks - 1:
            lo = (j // FFN_OUT_GROUP) * FFN_OUT_GROUP * CH
            for n in range(D_MODEL // CH):
                cols = slice(n * CH, (n + 1) * CH)
                part = _dot(act_ref[:, lo:(j + 1) * CH], wout_ref[lo:(j + 1) * CH, cols])
                o_ref[:, cols] = (x_ref[:, cols] if first else o_ref[:, cols]) + part
            first = False
            yield


def _swa_layer_kernel(sinks_ref, slopes_ref, h_ref, gain_ref, wq_ref, wo_ref, kp_ref, kc_ref, vtp_ref, vtc_ref,
                      kmeta_ref, vtmeta_ref, bias_ref, fgain_ref, fwin_ref, fwout_ref, *rest,
                      n_blocks, tiles_per_seq, final_norm):
    if final_norm:
        final_gain_ref, o_ref, *scratch = rest
    else:
        o_ref, *scratch = rest
    mid_ref, act_ref = scratch[-2:]
    i = pl.program_id(0)

    @pl.when(i == 0)
    def _():
        mid_ref[1] = jnp.zeros((mid_ref.shape[1], mid_ref.shape[2]), F32)

    t = lax.rem(jnp.minimum(i, pl.num_programs(0) - 2), tiles_per_seq)
    slot = lax.rem(i, 2)
    ffn = _ffn_steps(mid_ref.at[1 - slot], fgain_ref, fwin_ref, fwout_ref, act_ref, o_ref)
    mixer = _swa_steps(h_ref, t, sinks_ref, slopes_ref, gain_ref, wq_ref, wo_ref, kp_ref, kc_ref, vtp_ref, vtc_ref,
                       kmeta_ref, vtmeta_ref, bias_ref, *scratch[:-2], mid_ref.at[slot], n_blocks=n_blocks)
    next(ffn, None)
    for was_softmax in mixer:
        if was_softmax:
            next(ffn, None)
    for _ in ffn:
        pass
    if final_norm:
        o_ref[...] = _rms_norm(o_ref[...], final_gain_ref[...])


def _swa_layer(h, mix_gain, wq, wo, sinks, slopes, k, vt, k_meta, vt_meta, bias, ffn_gain, f_in, f_out, layer,
               swa_index, seq_len, *, tile, final_gain=None):
    T, D = h.shape
    W, G, KVH, KD, F = SWA_WINDOW, SWA_GROUP, SWA_KV_HEADS, SWA_KV_DIM, FFN_HIDDEN
    n_blocks = tile // W
    n_tiles = T // tile
    stride = (KVH + 1) * W
    final_norm = final_gain is not None
    smem = pl.BlockSpec(memory_space=pltpu.SMEM)
    cur = lambda i: jnp.minimum(i, n_tiles - 1)
    prev_block = lambda i: jnp.maximum(cur(i) * n_blocks - 1, 0)
    in_specs = [
        smem, smem,
        pl.BlockSpec((tile, D), lambda i: (cur(i), 0)),
        _resident_layer(layer, (1, D)),
        _resident_layer(swa_index, (D, SWA_Q_DIM)),
        _resident_layer(swa_index, (SWA_Q_DIM, D)),
        pl.BlockSpec((W, KD), lambda i: (prev_block(i), 0)),
        pl.BlockSpec((tile, KD), lambda i: (cur(i), 0)),
        pl.BlockSpec((KD, W), lambda i: (0, prev_block(i))),
        pl.BlockSpec((KD, tile), lambda i: (0, cur(i))),
        _resident((N_META, KD)),
        _resident((KD, W)),
        _resident((SWA_HEADS, 2 * W + N_META, W)),
        _resident_layer(layer, (1, D)),
        _resident_layer(layer, (D, 2 * F)),
        _resident_layer(layer, (F, D)),
    ]
    args = [sinks, slopes, h, mix_gain, wq, wo, k, k, vt, vt, k_meta, vt_meta, bias, ffn_gain, f_in, f_out]
    if final_norm:
        in_specs.append(_resident((1, D)))
        args.append(final_gain)
    return pl.pallas_call(
        functools.partial(_swa_layer_kernel, n_blocks=n_blocks, tiles_per_seq=seq_len // tile,
                          final_norm=final_norm),
        grid=(n_tiles + 1,),
        in_specs=in_specs,
        out_specs=pl.BlockSpec((tile, D), lambda i: (jnp.maximum(i - 1, 0), 0)),
        out_shape=jax.ShapeDtypeStruct((T, D), F32),
        scratch_shapes=[
            pltpu.VMEM((n_blocks, G * W, KD), BF16),
            pltpu.VMEM((n_blocks * stride + KVH * W, KD), BF16),
            pltpu.VMEM((KD + SWA_SUM_ROWS, n_blocks * stride + KVH * W), BF16),
            pltpu.VMEM((2, stride + KVH * W, G * W), F32),
            pltpu.VMEM((2, stride + KVH * W, G * W), BF16),
            pltpu.VMEM((SWA_Q_DIM, tile), BF16),
            pltpu.VMEM((2, tile, D), F32),
            pltpu.VMEM((tile, F), BF16),
        ],
        compiler_params=_params(1),
        name="swa_layer_final" if final_norm else "swa_layer",
    )(*args)


def _group_major(w, axis):
    shape = w.shape
    split = shape[:axis] + (SWA_KV_HEADS, SWA_GROUP, SWA_HEAD_DIM) + shape[axis + 1:]
    return jnp.swapaxes(w.reshape(split), axis, axis + 1).reshape(shape)


def kernel(x, meta_tokens, mix_norm, ffn_norm, ret_w_in, ret_w_out, kv_norm, kv_w, swa_w_q, swa_w_o,
           swa_sinks, ffn_w_in, ffn_w_out, final_norm):
    B, S, D = x.shape
    n_ret = ret_w_in.shape[0]
    n_swa = swa_w_q.shape[0]
    depth = n_ret + n_swa
    assert D == D_MODEL and S % TOKEN_TILE == 0

    meta_tables = _ret_tables(META_TILE)
    tables = _ret_tables(RET_MAIN_CHUNK)
    bias, slopes = (LOG2E * table for table in _swa_bias_tables())
    mix_gain = mix_norm.astype(F32).reshape(depth, 1, D)
    ffn_gain = ffn_norm.astype(F32).reshape(depth, 1, D)
    ret_in16, ret_out16 = ret_w_in.astype(BF16), ret_w_out.astype(BF16)
    ffn_in16, ffn_out16 = ffn_w_in.astype(BF16), ffn_w_out.astype(BF16)
    wq16 = _group_major(swa_w_q, 2).astype(BF16)
    wo16 = _group_major(swa_w_o, 1).astype(BF16)
    wk16 = kv_w[:, :SWA_KV_DIM].astype(BF16)
    wvt16 = kv_w[:, SWA_KV_DIM:].T.astype(BF16)
    kv_gain = kv_norm.astype(F32).reshape(1, D)

    hm = jnp.concatenate([jnp.zeros((META_TILE - N_META, D), F32), meta_tokens.astype(F32)], axis=0)
    h = x.astype(F32)
    state0 = jnp.zeros((RET_HEADS, RET_QK_DIM, RET_V_DIM), F32)
    k = vt = k_meta = vt_meta = None

    for layer in range(n_ret):
        hm, state_meta = _ret_mixer(hm[None], mix_gain, ret_in16, ret_out16, layer, meta_tables, state0,
                                    tile=META_TILE, emit_state=True)
        h = _ret_mixer(h, mix_gain, ret_in16, ret_out16, layer, tables, state_meta[0],
                       tile=TOKEN_TILE, emit_state=False)
        if layer == n_ret - 1:
            hm, k_meta, vt_meta = _ffn(hm[0], ffn_gain, ffn_in16, ffn_out16, layer, tile=META_TILE,
                                       kv_gain=kv_gain, wk=wk16, wvt=wvt16)
            h, k, vt = _ffn(h.reshape(B * S, D), ffn_gain, ffn_in16, ffn_out16, layer, tile=FFN_TILE,
                            kv_gain=kv_gain, wk=wk16, wvt=wvt16)
        else:
            hm = _ffn(hm[0], ffn_gain, ffn_in16, ffn_out16, layer, tile=META_TILE)
            h = _ffn(h.reshape(B * S, D), ffn_gain, ffn_in16, ffn_out16, layer, tile=FFN_TILE).reshape(B, S, D)

    k_meta = k_meta[META_TILE - N_META:]
    vt_meta = jnp.tile(vt_meta[:, META_TILE - N_META:], (1, SWA_WINDOW // N_META))
    for b in range(n_swa):
        layer = n_ret + b
        final_gain = final_norm.astype(F32).reshape(1, D) if layer == depth - 1 else None
        h = _swa_layer(h, mix_gain, wq16, wo16, LOG2E * swa_sinks[b].astype(F32), slopes, k, vt, k_meta, vt_meta, bias,
                       ffn_gain, ffn_in16, ffn_out16, layer, b, S, tile=TOKEN_TILE, final_gain=final_gain)
    return h.reshape(B, S, D)
```

```python
import functools

import jax
import jax.numpy as jnp
from jax import lax
from jax.experimental import pallas as pl
from jax.experimental.pallas import tpu as pltpu

D_MODEL = 1024
N_META = 16
RET_HEADS = 4
RET_QK_DIM = 256
RET_V_DIM = 512
RET_CHUNK = 128
RET_PROJ = 2 * RET_HEADS * RET_QK_DIM + 2 * RET_HEADS * RET_V_DIM
SWA_HEADS = 16
SWA_KV_HEADS = 4
SWA_GROUP = 4
SWA_HEAD_DIM = 64
SWA_WINDOW = 128
SWA_Q_DIM = SWA_HEADS * SWA_HEAD_DIM
SWA_KV_DIM = SWA_KV_HEADS * SWA_HEAD_DIM
SWA_SUM_ROWS = 16
FFN_HIDDEN = 2816
FFN_COL_CHUNK = 256
FFN_OUT_GROUP = 4
RET_PROJ_COL_CHUNK = 256
RMS_EPS = 1e-6
GN_EPS = 1e-6

TOKEN_TILE = 512
FFN_TILE = 1024
META_TILE = RET_CHUNK
RET_MAIN_CHUNK = 256
VMEM_LIMIT_BYTES = 56 * 1024 * 1024

BF16 = jnp.bfloat16
F32 = jnp.float32
NEG_INF = float("-inf")
LOG2E = 1.4426950408889634


def _resident(block_shape):
    zeros = (0,) * len(block_shape)
    return pl.BlockSpec(block_shape, lambda *_: zeros, pipeline_mode=pl.Buffered(1))


def _resident_layer(layer, tail_shape):
    index = (layer,) + (0,) * len(tail_shape)
    return pl.BlockSpec((None,) + tuple(tail_shape), lambda *_: index, pipeline_mode=pl.Buffered(1))


def _rms_norm(x, gain):
    ms = jnp.mean(x * x, axis=-1, keepdims=True)
    return x * lax.rsqrt(ms + RMS_EPS) * gain


def _silu(x):
    half = 0.5 * x
    return half + half * jnp.tanh(half)


def _dot(a, b):
    return jnp.dot(a, b, preferred_element_type=F32)


def _dot_nt(a, b):
    return lax.dot_general(a, b, (((1,), (1,)), ((), ())), preferred_element_type=F32)


def _dot_tn(a, b):
    return lax.dot_general(a, b, (((0,), (0,)), ((), ())), preferred_element_type=F32)


def _params(n_grid_axes):
    return pltpu.CompilerParams(dimension_semantics=("arbitrary",) * n_grid_axes,
                                vmem_limit_bytes=VMEM_LIMIT_BYTES)


def _ret_tables(C):
    log_gamma = jnp.log1p(-jnp.exp2(-5.0 - jnp.arange(RET_HEADS, dtype=F32)))
    i = jnp.arange(C, dtype=F32)
    diff = i[:, None] - i[None, :]
    dmask = jnp.where(diff >= 0, jnp.exp(log_gamma[:, None, None] * jnp.maximum(diff, 0.0)), 0.0)
    zeta = jnp.exp(log_gamma[:, None] * (C - 1.0 - i)[None, :])
    xi = jnp.exp(log_gamma[:, None] * (i + 1.0)[None, :])
    chunk_decay = jnp.exp(log_gamma * C)
    xi = jnp.broadcast_to(xi[:, :, None], (RET_HEADS, C, RET_QK_DIM))
    zeta = jnp.broadcast_to(zeta[:, :, None], (RET_HEADS, C, RET_QK_DIM))
    cd = jnp.broadcast_to(chunk_decay[:, None, None], (RET_HEADS, 8, RET_V_DIM))
    return dmask, xi, zeta, cd


def _ret_kernel(h_ref, gain_ref, win_ref, wout_ref, dmask_ref, xi_ref, zeta_ref, cd_ref, state0_ref,
                *rest, chunk, n_chunks, emit_state):
    if emit_state:
        o_ref, state_out_ref, state_ref, proj_ref, og_ref = rest
    else:
        o_ref, state_ref, proj_ref, og_ref = rest
    H, dk, dv, C = RET_HEADS, RET_QK_DIM, RET_V_DIM, chunk
    t = pl.program_id(1)

    @pl.when(t == 0)
    def _():
        state_ref[...] = state0_ref[...]

    xn = _rms_norm(h_ref[0], gain_ref[...]).astype(BF16)
    scale = dk ** -0.5
    NC = RET_PROJ_COL_CHUNK

    def head_cols(hh):
        v0, g0 = 2 * H * dk, 2 * H * dk + H * dv
        return (slice(hh * dk, (hh + 1) * dk), slice(H * dk + hh * dk, H * dk + (hh + 1) * dk),
                slice(v0 + hh * dv, v0 + (hh + 1) * dv), slice(g0 + hh * dv, g0 + (hh + 1) * dv))

    def in_proj(hh):
        for cols in head_cols(hh):
            for lo in range(cols.start, cols.stop, NC):
                proj_ref[:, lo:lo + NC] = _dot(xn, win_ref[:, lo:lo + NC])

    def core(hh):
        qc, kc, vc, gc = head_cols(hh)
        for c in range(n_chunks):
            rows = slice(c * C, (c + 1) * C)
            q = proj_ref[rows, qc] * scale
            k = proj_ref[rows, kc]
            v = proj_ref[rows, vc].astype(BF16)
            gate = proj_ref[rows, gc]
            scores = _dot_nt(q.astype(BF16), k.astype(BF16)) * dmask_ref[hh]
            intra = _dot(scores.astype(BF16), v)
            state = state_ref[hh]
            inter = _dot((q * xi_ref[hh]).astype(BF16), state.astype(BF16))
            kz = (k * zeta_ref[hh]).astype(BF16)
            state_ref[hh] = state * cd_ref[hh][0:1, :] + _dot_tn(kz, v)
            o = intra + inter
            mu = jnp.mean(o, axis=-1, keepdims=True)
            oc = o - mu
            var = jnp.mean(oc * oc, axis=-1, keepdims=True)
            on = oc * lax.rsqrt(var + GN_EPS)
            og_ref[rows, hh * dv:(hh + 1) * dv] = (_silu(gate) * on).astype(BF16)

    in_proj(0)
    for hh in range(H):
        if hh + 1 < H:
            in_proj(hh + 1)
        core(hh)
    for lo in range(0, D_MODEL, NC):
        o_ref[0, :, lo:lo + NC] = h_ref[0, :, lo:lo + NC] + _dot(og_ref[...], wout_ref[:, lo:lo + NC])

    if emit_state:
        @pl.when(t == pl.num_programs(1) - 1)
        def _():
            state_out_ref[0] = state_ref[...]


def _ret_mixer(h, gain, w_in, w_out, layer, tables, state0, *, tile, emit_state):
    B, L, D = h.shape
    H, dk, dv = RET_HEADS, RET_QK_DIM, RET_V_DIM
    dmask, xi, zeta, cd = tables
    C = dmask.shape[-1]
    out_shape = [jax.ShapeDtypeStruct((B, L, D), F32)]
    out_specs = [pl.BlockSpec((1, tile, D), lambda b, t: (b, t, 0))]
    if emit_state:
        out_shape.append(jax.ShapeDtypeStruct((B, H, dk, dv), F32))
        out_specs.append(pl.BlockSpec((1, H, dk, dv), lambda b, t: (b, 0, 0, 0)))
    res = pl.pallas_call(
        functools.partial(_ret_kernel, chunk=C, n_chunks=tile // C, emit_state=emit_state),
        grid=(B, L // tile),
        in_specs=[
            pl.BlockSpec((1, tile, D), lambda b, t: (b, t, 0)),
            _resident_layer(layer, (1, D)),
            _resident_layer(layer, (D, RET_PROJ)),
            _resident_layer(layer, (H * dv, D)),
            _resident((H, C, C)),
            _resident((H, C, dk)),
            _resident((H, C, dk)),
            _resident((H, 8, dv)),
            _resident((H, dk, dv)),
        ],
        out_specs=out_specs,
        out_shape=out_shape,
        scratch_shapes=[
            pltpu.VMEM((H, dk, dv), F32),
            pltpu.VMEM((tile, RET_PROJ), F32),
            pltpu.VMEM((tile, H * dv), BF16),
        ],
        compiler_params=_params(2),
        name="ret_mixer_meta" if emit_state else "ret_mixer",
    )(h, gain, w_in, w_out, dmask, xi, zeta, cd, state0)
    return res if emit_state else res[0]


def _ffn_body(x, gain_ref, win_ref, wout_ref, act_ref):
    F, CH = FFN_HIDDEN, FFN_COL_CHUNK
    xn = _rms_norm(x, gain_ref[...]).astype(BF16)
    for j in range(F // CH):
        gate = _dot(xn, win_ref[:, j * CH:(j + 1) * CH])
        up = _dot(xn, win_ref[:, F + j * CH:F + (j + 1) * CH])
        act_ref[:, j * CH:(j + 1) * CH] = (_silu(gate) * up).astype(BF16)
    return x + _dot(act_ref[...], wout_ref[...])


def _ffn_kernel(h_ref, gain_ref, win_ref, wout_ref, *rest, with_kv):
    if with_kv:
        kv_gain_ref, wk_ref, wvt_ref, o_ref, k_ref, vt_ref, act_ref = rest
    else:
        o_ref, act_ref = rest
    y = _ffn_body(h_ref[...], gain_ref, win_ref, wout_ref, act_ref)
    o_ref[...] = y
    if with_kv:
        xn = _rms_norm(y, kv_gain_ref[...]).astype(BF16)
        k_ref[...] = _dot(xn, wk_ref[...]).astype(BF16)
        vt_ref[...] = _dot_nt(wvt_ref[...], xn).astype(BF16)


def _ffn(h, gain, w_in, w_out, layer, *, tile, kv_gain=None, wk=None, wvt=None):
    T, D = h.shape
    F, KD = FFN_HIDDEN, SWA_KV_DIM
    with_kv = wk is not None
    in_specs = [
        pl.BlockSpec((tile, D), lambda t: (t, 0)),
        _resident_layer(layer, (1, D)),
        _resident_layer(layer, (D, 2 * F)),
        _resident_layer(layer, (F, D)),
    ]
    args = [h, gain, w_in, w_out]
    out_shape = [jax.ShapeDtypeStruct((T, D), F32)]
    out_specs = [pl.BlockSpec((tile, D), lambda t: (t, 0))]
    if with_kv:
        in_specs += [_resident((1, D)), _resident((D, KD)), _resident((KD, D))]
        args += [kv_gain, wk, wvt]
        out_shape += [jax.ShapeDtypeStruct((T, KD), BF16), jax.ShapeDtypeStruct((KD, T), BF16)]
        out_specs += [pl.BlockSpec((tile, KD), lambda t: (t, 0)), pl.BlockSpec((KD, tile), lambda t: (0, t))]
    res = pl.pallas_call(
        functools.partial(_ffn_kernel, with_kv=with_kv),
        grid=(T // tile,),
        in_specs=in_specs,
        out_specs=out_specs,
        out_shape=out_shape,
        scratch_shapes=[pltpu.VMEM((tile, F), BF16)],
        compiler_params=_params(1),
        name="ffn_kv" if with_kv else "ffn",
    )(*args)
    return res if with_kv else res[0]


def _swa_bias_tables():
    W = SWA_WINDOW
    slopes = jnp.exp2(-8.0 * (jnp.arange(SWA_HEADS, dtype=F32) + 1.0) / SWA_HEADS)
    c = jnp.arange(W)[:, None]
    r = jnp.arange(W)[None, :]
    s = slopes[:, None, None]
    b_prev = jnp.where((c > r)[None], -s * (W + r - c).astype(F32)[None], NEG_INF)
    b_cur = jnp.where((c <= r)[None], -s * (r - c).astype(F32)[None], NEG_INF)
    m = jnp.arange(N_META)[:, None]
    b_meta = -s * (N_META + r - m).astype(F32)[None]
    return jnp.concatenate([b_prev, b_meta, b_cur], axis=1), slopes


def _swa_steps(h_ref, t, sinks_ref, slopes_ref, gain_ref, wq_ref, wo_ref, kp_ref, kc_ref, vtp_ref, vtc_ref,
               kmeta_ref, vtmeta_ref, bias_ref, q_ref, km_ref, vmt_ref, st_ref, pt_ref, attt_ref,
               mid_ref, *, n_blocks):
    W, G, KVH, KD, HD = SWA_WINDOW, SWA_GROUP, SWA_KV_HEADS, SWA_KV_DIM, SWA_HEAD_DIM
    SUM_ROWS = SWA_SUM_ROWS
    BLK = KVH * W
    MET = W
    STRIDE = BLK + MET
    WIN = 2 * BLK + MET
    half_blocks = n_blocks // 2

    def q_proj(first_block):
        rows = slice(first_block * W, (first_block + half_blocks) * W)
        xn = _rms_norm(h_ref[rows, :], gain_ref[...]).astype(BF16)
        for g in range(G):
            qg = (_dot(xn, wq_ref[:, g * KD:(g + 1) * KD]) * (HD ** -0.5 * LOG2E)).astype(BF16)
            for bb in range(half_blocks):
                q_ref[first_block + bb, g * W:(g + 1) * W, :] = qg[bb * W:(bb + 1) * W, :]

    def o_proj(first_block):
        rows = slice(first_block * W, (first_block + half_blocks) * W)
        mid_ref[rows, :] = h_ref[rows, :] + _dot_tn(attt_ref[:, rows], wo_ref[...])

    q_proj(0)
    yield False

    col_kv = lax.broadcasted_iota(jnp.int32, (W, KD), 1) // HD
    col_kv_meta = lax.broadcasted_iota(jnp.int32, (N_META, KD), 1) // HD
    row_kv = lax.broadcasted_iota(jnp.int32, (KD, W), 0) // HD
    lane = lax.broadcasted_iota(jnp.int32, (KD, W), 1)
    zero = jnp.zeros((), BF16)
    kmeta_bd = jnp.concatenate(
        [jnp.where(col_kv_meta == kv, kmeta_ref[...], zero) for kv in range(KVH)]
        + [jnp.zeros((MET - KVH * N_META, KD), BF16)], axis=0)
    vtmeta_bd = jnp.where((lane // N_META == row_kv) & (lane < KVH * N_META), vtmeta_ref[...], zero)
    ind_row = lax.broadcasted_iota(jnp.int32, (SUM_ROWS, W), 0)
    ind_lane = lax.broadcasted_iota(jnp.int32, (SUM_ROWS, W), 1)
    ind_meta = ((ind_lane // N_META == ind_row) & (ind_lane < KVH * N_META)).astype(BF16)
    for j in range(n_blocks + 1):
        kblk = kp_ref[...] if j == 0 else kc_ref[(j - 1) * W:j * W, :]
        vblk = vtp_ref[...] if j == 0 else vtc_ref[:, (j - 1) * W:j * W]
        for kv in range(KVH):
            span = slice(j * STRIDE + kv * W, j * STRIDE + (kv + 1) * W)
            km_ref[span, :] = jnp.where(col_kv == kv, kblk, zero)
            vmt_ref[:KD, span] = jnp.where(row_kv == kv, vblk, zero)
            vmt_ref[KD:, span] = (ind_row == kv).astype(BF16)
        if j < n_blocks:
            span = slice(j * STRIDE + BLK, (j + 1) * STRIDE)
            km_ref[span, :] = kmeta_bd
            vmt_ref[:KD, span] = vtmeta_bd
            vmt_ref[KD:, span] = ind_meta
    yield False

    def logits(b):
        window = slice(b * STRIDE, b * STRIDE + WIN)
        st_ref[b % 2] = _dot_nt(km_ref[window, :], q_ref[b])
        pt_ref[b % 2, BLK + KVH * N_META:STRIDE, :] = jnp.zeros((MET - KVH * N_META, G * W), BF16)

    first_pen = jnp.where(t == 0, NEG_INF, 0.0).astype(F32)
    for b in range(n_blocks):
        if b == half_blocks:
            q_proj(half_blocks)
            yield False
        logits(b)
        yield False
        slot = b % 2
        window = slice(b * STRIDE, b * STRIDE + WIN)
        block_index = (t * n_blocks + b).astype(F32)
        sink_terms = []
        for kv in range(KVH):
            prev_rows = slice(kv * W, (kv + 1) * W)
            meta_rows = slice(BLK + kv * N_META, BLK + (kv + 1) * N_META)
            cur_rows = slice(STRIDE + kv * W, STRIDE + (kv + 1) * W)
            sink_row = []
            for g in range(G):
                head = kv * G + g
                cols = slice(g * W, (g + 1) * W)
                sink = sinks_ref[head]
                meta_shift = -(slopes_ref[head] * float(W)) * block_index
                s_prev = st_ref[slot, prev_rows, cols] + bias_ref[head, 0:W, :]
                if b == 0:
                    s_prev = s_prev + first_pen
                s_meta = st_ref[slot, meta_rows, cols] + (bias_ref[head, W:W + N_META, :] + meta_shift)
                s_cur = st_ref[slot, cur_rows, cols] + bias_ref[head, W + N_META:, :]
                m = jnp.maximum(jnp.maximum(jnp.max(s_prev, axis=0, keepdims=True),
                                            jnp.max(s_cur, axis=0, keepdims=True)),
                                jnp.max(s_meta, axis=0, keepdims=True))
                m = jnp.maximum(m, sink)
                pt_ref[slot, prev_rows, cols] = jnp.exp2(s_prev - m).astype(BF16)
                pt_ref[slot, meta_rows, cols] = jnp.exp2(s_meta - m).astype(BF16)
                pt_ref[slot, cur_rows, cols] = jnp.exp2(s_cur - m).astype(BF16)
                sink_row.append(jnp.exp2(sink - m))
            sink_terms.append(jnp.concatenate(sink_row, axis=1))
            yield True
        out_all = _dot(vmt_ref[:, window], pt_ref[slot])
        inv = 1.0 / (out_all[KD:KD + KVH, :] + jnp.concatenate(sink_terms, axis=0))
        for g in range(G):
            cols = slice(g * W, (g + 1) * W)
            out_g = jnp.concatenate([out_all[kv * HD:(kv + 1) * HD, cols] * inv[kv:kv + 1, cols]
                                     for kv in range(KVH)], axis=0)
            attt_ref[g * KD:(g + 1) * KD, b * W:(b + 1) * W] = out_g.astype(BF16)
        yield False
        if b + 1 == half_blocks:
            o_proj(0)
            yield False
    o_proj(half_blocks)
    yield False


def _ffn_steps(x_ref, gain_ref, win_ref, wout_ref, act_ref, o_ref):
    F, CH = FFN_HIDDEN, FFN_COL_CHUNK
    n_chunks = F // CH
    xn = _rms_norm(x_ref[...], gain_ref[...]).astype(BF16)
    first = True
    for j in range(n_chunks):
        gate = _dot(xn, win_ref[:, j * CH:(j + 1) * CH])
        up = _dot(xn, win_ref[:, F + j * CH:F + (j + 1) * CH])
        act_ref[:, j * CH:(j + 1) * CH] = (_silu(gate) * up).astype(BF16)
        yield
        if (j + 1) % FFN_OUT_GROUP == 0 or j == n_chunks - 1:
            lo = (j // FFN_OUT_GROUP) * FFN_OUT_GROUP * CH
            for n in range(D_MODEL // CH):
                cols = slice(n * CH, (n + 1) * CH)
                part = _dot(act_ref[:, lo:(j + 1) * CH], wout_ref[lo:(j + 1) * CH, cols])
                o_ref[:, cols] = (x_ref[:, cols] if first else o_ref[:, cols]) + part
            first = False
            yield


def _swa_layer_kernel(sinks_ref, slopes_ref, h_ref, gain_ref, wq_ref, wo_ref, kp_ref, kc_ref, vtp_ref, vtc_ref,
                      kmeta_ref, vtmeta_ref, bias_ref, fgain_ref, fwin_ref, fwout_ref, *rest,
                      n_blocks, tiles_per_seq, final_norm):
    if final_norm:
        final_gain_ref, o_ref, *scratch = rest
    else:
        o_ref, *scratch = rest
    mid_ref, act_ref = scratch[-2:]
    i = pl.program_id(0)

    @pl.when(i == 0)
    def _():
        mid_ref[1] = jnp.zeros((mid_ref.shape[1], mid_ref.shape[2]), F32)

    t = lax.rem(jnp.minimum(i, pl.num_programs(0) - 2), tiles_per_seq)
    slot = lax.rem(i, 2)
    ffn = _ffn_steps(mid_ref.at[1 - slot], fgain_ref, fwin_ref, fwout_ref, act_ref, o_ref)
    mixer = _swa_steps(h_ref, t, sinks_ref, slopes_ref, gain_ref, wq_ref, wo_ref, kp_ref, kc_ref, vtp_ref, vtc_ref,
                       kmeta_ref, vtmeta_ref, bias_ref, *scratch[:-2], mid_ref.at[slot], n_blocks=n_blocks)
    next(ffn, None)
    for was_softmax in mixer:
        if was_softmax:
            next(ffn, None)
    for _ in ffn:
        pass
    if final_norm:
        o_ref[...] = _rms_norm(o_ref[...], final_gain_ref[...])


def _swa_layer(h, mix_gain, wq, wo, sinks, slopes, k, vt, k_meta, vt_meta, bias, ffn_gain, f_in, f_out, layer,
               swa_index, seq_len, *, tile, final_gain=None):
    T, D = h.shape
    W, G, KVH, KD, F = SWA_WINDOW, SWA_GROUP, SWA_KV_HEADS, SWA_KV_DIM, FFN_HIDDEN
    n_blocks = tile // W
    n_tiles = T // tile
    stride = (KVH + 1) * W
    final_norm = final_gain is not None
    smem = pl.BlockSpec(memory_space=pltpu.SMEM)
    cur = lambda i: jnp.minimum(i, n_tiles - 1)
    prev_block = lambda i: jnp.maximum(cur(i) * n_blocks - 1, 0)
    in_specs = [
        smem, smem,
        pl.BlockSpec((tile, D), lambda i: (cur(i), 0)),
        _resident_layer(layer, (1, D)),
        _resident_layer(swa_index, (D, SWA_Q_DIM)),
        _resident_layer(swa_index, (SWA_Q_DIM, D)),
        pl.BlockSpec((W, KD), lambda i: (prev_block(i), 0)),
        pl.BlockSpec((tile, KD), lambda i: (cur(i), 0)),
        pl.BlockSpec((KD, W), lambda i: (0, prev_block(i))),
        pl.BlockSpec((KD, tile), lambda i: (0, cur(i))),
        _resident((N_META, KD)),
        _resident((KD, W)),
        _resident((SWA_HEADS, 2 * W + N_META, W)),
        _resident_layer(layer, (1, D)),
        _resident_layer(layer, (D, 2 * F)),
        _resident_layer(layer, (F, D)),
    ]
    args = [sinks, slopes, h, mix_gain, wq, wo, k, k, vt, vt, k_meta, vt_meta, bias, ffn_gain, f_in, f_out]
    if final_norm:
        in_specs.append(_resident((1, D)))
        args.append(final_gain)
    return pl.pallas_call(
        functools.partial(_swa_layer_kernel, n_blocks=n_blocks, tiles_per_seq=seq_len // tile,
                          final_norm=final_norm),
        grid=(n_tiles + 1,),
        in_specs=in_specs,
        out_specs=pl.BlockSpec((tile, D), lambda i: (jnp.maximum(i - 1, 0), 0)),
        out_shape=jax.ShapeDtypeStruct((T, D), F32),
        scratch_shapes=[
            pltpu.VMEM((n_blocks, G * W, KD), BF16),
            pltpu.VMEM((n_blocks * stride + KVH * W, KD), BF16),
            pltpu.VMEM((KD + SWA_SUM_ROWS, n_blocks * stride + KVH * W), BF16),
            pltpu.VMEM((2, stride + KVH * W, G * W), F32),
            pltpu.VMEM((2, stride + KVH * W, G * W), BF16),
            pltpu.VMEM((SWA_Q_DIM, tile), BF16),
            pltpu.VMEM((2, tile, D), F32),
            pltpu.VMEM((tile, F), BF16),
        ],
        compiler_params=_params(1),
        name="swa_layer_final" if final_norm else "swa_layer",
    )(*args)


def _group_major(w, axis):
    shape = w.shape
    split = shape[:axis] + (SWA_KV_HEADS, SWA_GROUP, SWA_HEAD_DIM) + shape[axis + 1:]
    return jnp.swapaxes(w.reshape(split), axis, axis + 1).reshape(shape)


def kernel(x, meta_tokens, mix_norm, ffn_norm, ret_w_in, ret_w_out, kv_norm, kv_w, swa_w_q, swa_w_o,
           swa_sinks, ffn_w_in, ffn_w_out, final_norm):
    B, S, D = x.shape
    n_ret = ret_w_in.shape[0]
    n_swa = swa_w_q.shape[0]
    depth = n_ret + n_swa
    assert D == D_MODEL and S % TOKEN_TILE == 0

    meta_tables = _ret_tables(META_TILE)
    tables = _ret_tables(RET_MAIN_CHUNK)
    bias, slopes = (LOG2E * table for table in _swa_bias_tables())
    mix_gain = mix_norm.astype(F32).reshape(depth, 1, D)
    ffn_gain = ffn_norm.astype(F32).reshape(depth, 1, D)
    ret_in16, ret_out16 = ret_w_in.astype(BF16), ret_w_out.astype(BF16)
    ffn_in16, ffn_out16 = ffn_w_in.astype(BF16), ffn_w_out.astype(BF16)
    wq16 = _group_major(swa_w_q, 2).astype(BF16)
    wo16 = _group_major(swa_w_o, 1).astype(BF16)
    wk16 = kv_w[:, :SWA_KV_DIM].astype(BF16)
    wvt16 = kv_w[:, SWA_KV_DIM:].T.astype(BF16)
    kv_gain = kv_norm.astype(F32).reshape(1, D)

    hm = jnp.concatenate([jnp.zeros((META_TILE - N_META, D), F32), meta_tokens.astype(F32)], axis=0)
    h = x.astype(F32)
    state0 = jnp.zeros((RET_HEADS, RET_QK_DIM, RET_V_DIM), F32)
    k = vt = k_meta = vt_meta = None

    for layer in range(n_ret):
        hm, state_meta = _ret_mixer(hm[None], mix_gain, ret_in16, ret_out16, layer, meta_tables, state0,
                                    tile=META_TILE, emit_state=True)
        h = _ret_mixer(h, mix_gain, ret_in16, ret_out16, layer, tables, state_meta[0],
                       tile=TOKEN_TILE, emit_state=False)
        if layer == n_ret - 1:
            hm, k_meta, vt_meta = _ffn(hm[0], ffn_gain, ffn_in16, ffn_out16, layer, tile=META_TILE,
                                       kv_gain=kv_gain, wk=wk16, wvt=wvt16)
            h, k, vt = _ffn(h.reshape(B * S, D), ffn_gain, ffn_in16, ffn_out16, layer, tile=FFN_TILE,
                            kv_gain=kv_gain, wk=wk16, wvt=wvt16)
        else:
            hm = _ffn(hm[0], ffn_gain, ffn_in16, ffn_out16, layer, tile=META_TILE)
            h = _ffn(h.reshape(B * S, D), ffn_gain, ffn_in16, ffn_out16, layer, tile=FFN_TILE).reshape(B, S, D)

    k_meta = k_meta[META_TILE - N_META:]
    vt_meta = jnp.tile(vt_meta[:, META_TILE - N_META:], (1, SWA_WINDOW // N_META))
    for b in range(n_swa):
        layer = n_ret + b
        final_gain = final_norm.astype(F32).reshape(1, D) if layer == depth - 1 else None
        h = _swa_layer(h, mix_gain, wq16, wo16, LOG2E * swa_sinks[b].astype(F32), slopes, k, vt, k_meta, vt_meta, bias,
                       ffn_gain, ffn_in16, ffn_out16, layer, b, S, tile=TOKEN_TILE, final_gain=final_gain)
    return h.reshape(B, S, D)
```
